```python
import math
import jax, jax.numpy as jnp
from jax import lax
import numpy as np


D_MODEL = 2048
BATCH = 1
SEQ = 8192
DEPTH = 4
DEC_BATCH = 16
DEC_SEQ = 64
PAST_LEN = 4096

CHUNK = 64
Q_BLOCK = 128
D_MIX = D_MODEL
C_RWKV = D_MIX // 2
RWKV_HEAD = 64
H_RWKV = C_RWKV // RWKV_HEAD
D_DECAY_LORA = 64
D_AAA_LORA = 64
C_DIFF = D_MIX - C_RWKV
H_DIFF = 8
DV_DIFF = C_DIFF // H_DIFF
DK_DIFF = DV_DIFF // 2
D_SHIFT = 3 * C_RWKV + D_DECAY_LORA + D_AAA_LORA
D_IN = D_SHIFT + C_RWKV + 3 * C_DIFF + C_DIFF
NORM_EPS = 1e-6
RWKV_LN_EPS = 64e-5
SUBLN_EPS = 1e-5
NEG_INF = -1e30

kernel_name = 'hymba_rwkv7_diffattn_stream_step'


def rmsnorm(x, g, eps=NORM_EPS):
    x32 = x.astype(jnp.float32)
    y = x32 * lax.rsqrt(jnp.mean(x32 * x32, axis=-1, keepdims=True) + eps)
    return (y * g.astype(jnp.float32)).astype(x.dtype)


def lambda_init(l):
    return 0.8 - 0.6 * math.exp(-0.3 * l)


def alibi_slopes():
    return jnp.exp2(-8.0 * jnp.arange(1, H_DIFF + 1, dtype=jnp.float32) / H_DIFF)


def diff_block(q, k, v, q_pos, k_pos, lam):
    s = jnp.einsum('bqhcd,bshcd->bhcqs', q.astype(jnp.float32), k.astype(jnp.float32)) * (DK_DIFF ** -0.5)
    dist = jnp.abs(q_pos[:, None] - k_pos[None, :]).astype(jnp.float32)
    s = s - alibi_slopes()[None, :, None, None, None] * dist
    visible = (k_pos[None, :] // CHUNK) <= (q_pos[:, None] // CHUNK)
    s = jnp.where(visible, s, NEG_INF)
    p = jax.nn.softmax(s, axis=-1)
    attn = p[:, :, 0] - lam * p[:, :, 1]
    return jnp.einsum('bhqs,bshv->bqhv', attn, v.astype(jnp.float32))


def diff_attention(q, k, v, q_pos, k_pos, lam):
    B, T = q.shape[0], q.shape[1]
    qb = Q_BLOCK if T % Q_BLOCK == 0 else T
    nb = T // qb
    q_blocks = jnp.moveaxis(q.reshape(B, nb, qb, H_DIFF, 2, DK_DIFF), 1, 0)
    pos_blocks = q_pos.reshape(nb, qb)
    out = lax.map(lambda a: diff_block(a[0], k, v, a[1], k_pos, lam), (q_blocks, pos_blocks))
    return jnp.moveaxis(out, 0, 1).reshape(B, T, H_DIFF, DV_DIFF)


def rwkv_scan(S0, r, w, k, v, kk, a):
    def step(S, inp):
        r_t, w_t, k_t, v_t, kk_t, a_t = inp
        sa = jnp.einsum('bhvk,bhk->bhv', S, -kk_t)
        S = S * w_t[:, :, None, :] + sa[..., None] * (kk_t * a_t)[:, :, None, :] + v_t[..., None] * k_t[:, :, None, :]
        return S, jnp.einsum('bhvk,bhk->bhv', S, r_t)
    seq = tuple(jnp.moveaxis(t.astype(jnp.float32), 1, 0) for t in (r, w, k, v, kk, a))
    S_T, y = lax.scan(step, S0.astype(jnp.float32), seq)
    return S_T, jnp.moveaxis(y, 0, 1)


def mixer_layer(x, l, past_k, past_v, wkv0, prev0, norm_g, w_in, shift_mu, w_decay0, w_decay2, a0, w_a2,
                k_k, k_a, r_k, lnx_w, lnx_b, lam_q1, lam_k1, lam_q2, lam_k2, subln_g, w_out):
    f32 = jnp.float32
    B, T = x.shape[0], x.shape[1]
    past_len = past_k.shape[1]
    h = rmsnorm(x, norm_g)
    proj = jnp.einsum('btd,de->bte', h, w_in)
    cuts = [D_SHIFT, D_SHIFT + C_RWKV, D_SHIFT + C_RWKV + C_DIFF,
            D_SHIFT + C_RWKV + 2 * C_DIFF, D_SHIFT + C_RWKV + 3 * C_DIFF]
    sh, g_r, q, kd, vd, g_d = jnp.split(proj, cuts, axis=-1)

    prev = jnp.concatenate([prev0[:, None, :].astype(sh.dtype), sh[:, :-1]], axis=1)
    xs = sh + (prev - sh) * shift_mu
    new_prev = sh[:, -1]
    r, kr, vr, wlo, alo = jnp.split(xs, [C_RWKV, 2 * C_RWKV, 3 * C_RWKV, 3 * C_RWKV + D_DECAY_LORA], axis=-1)
    heads = lambda t: t.reshape(B, T, H_RWKV, RWKV_HEAD)
    wlog = -jax.nn.softplus(-(w_decay0.astype(f32) + jnp.tanh(wlo.astype(f32)) @ w_decay2.astype(f32))) - 0.5
    decay = jnp.exp(-jnp.exp(wlog))
    a = jax.nn.sigmoid(a0.astype(f32) + alo.astype(f32) @ w_a2.astype(f32))
    r32, kr32, v32 = r.astype(f32), kr.astype(f32), vr.astype(f32)
    kk = heads(kr32 * k_k.astype(f32))
    kk = kk / jnp.maximum(jnp.sqrt(jnp.sum(kk * kk, axis=-1, keepdims=True)), 1e-12)
    kw = kr32 * (1.0 + (a - 1.0) * k_a.astype(f32))
    S_T, y = rwkv_scan(wkv0, heads(r32), heads(decay), heads(kw), heads(v32), kk, heads(a))
    mu = jnp.mean(y, axis=-1, keepdims=True)
    var = jnp.mean(jnp.square(y - mu), axis=-1, keepdims=True)
    yn = ((y - mu) * lax.rsqrt(var + RWKV_LN_EPS)).reshape(B, T, C_RWKV) * lnx_w.astype(f32) + lnx_b.astype(f32)
    bonus = jnp.sum(heads(r32) * heads(kw) * r_k.astype(f32), axis=-1, keepdims=True) * heads(v32)
    y_rwkv = (yn + bonus.reshape(B, T, C_RWKV)) * jax.nn.silu(g_r.astype(f32))

    qh = q.reshape(B, T, H_DIFF, 2, DK_DIFF)
    kh = kd.reshape(B, T, H_DIFF, 2 * DK_DIFF)
    vh = vd.reshape(B, T, H_DIFF, DV_DIFF)
    k_all = jnp.concatenate([past_k.astype(kh.dtype), kh], axis=1).reshape(B, past_len + T, H_DIFF, 2, DK_DIFF)
    v_all = jnp.concatenate([past_v.astype(vh.dtype), vh], axis=1)
    q_pos = past_len + jnp.arange(T, dtype=jnp.int32)
    k_pos = jnp.arange(past_len + T, dtype=jnp.int32)
    lam_init = lambda_init(l)
    lam = (jnp.exp(jnp.sum(lam_q1.astype(f32) * lam_k1.astype(f32)))
           - jnp.exp(jnp.sum(lam_q2.astype(f32) * lam_k2.astype(f32))) + lam_init)
    o_d = diff_attention(qh, k_all, v_all, q_pos, k_pos, lam)
    o_d = o_d * lax.rsqrt(jnp.mean(o_d * o_d, axis=-1, keepdims=True) + SUBLN_EPS) * subln_g.astype(f32) * (1.0 - lam_init)
    y_diff = o_d.reshape(B, T, C_DIFF) * jax.nn.silu(g_d.astype(f32))

    mix = jnp.concatenate([y_rwkv, y_diff], axis=-1).astype(x.dtype)
    out = x + jnp.einsum('bte,ed->btd', mix, w_out)
    return out, kh, vh, S_T.astype(wkv0.dtype), new_prev


def setup_inputs(seed: int = 0) -> dict:
    key = jax.random.key(seed)
    ks = jax.random.split(key, 26)
    f32 = jnp.float32
    nrm = lambda k, shape, s: jax.random.normal(k, shape, f32) * s
    return {
        'x_prompt': nrm(ks[0], (BATCH, SEQ, D_MODEL), 1.0),
        'x_sample': nrm(ks[1], (DEC_BATCH, DEC_SEQ, D_MODEL), 1.0),
        'cache_k': nrm(ks[2], (DEPTH, DEC_BATCH, PAST_LEN, H_DIFF, 2 * DK_DIFF), 1.0),
        'cache_v': nrm(ks[3], (DEPTH, DEC_BATCH, PAST_LEN, H_DIFF, DV_DIFF), 1.0),
        'state_wkv': nrm(ks[4], (DEPTH, DEC_BATCH, H_RWKV, RWKV_HEAD, RWKV_HEAD), 0.5),
        'state_shift': nrm(ks[5], (DEPTH, DEC_BATCH, D_SHIFT), 1.0),
        'norm_g': 1.0 + nrm(ks[6], (DEPTH, D_MODEL), 0.02),
        'w_in': nrm(ks[7], (DEPTH, D_MODEL, D_IN), D_MODEL ** -0.5),
        'shift_mu': jax.random.uniform(ks[8], (DEPTH, D_SHIFT), f32, 0.0, 1.0),
        'w_decay0': jax.random.uniform(ks[9], (DEPTH, C_RWKV), f32, -3.0, -1.0),
        'w_decay2': nrm(ks[10], (DEPTH, D_DECAY_LORA, C_RWKV), 0.1 * D_DECAY_LORA ** -0.5),
        'a0': nrm(ks[11], (DEPTH, C_RWKV), 0.1),
        'w_a2': nrm(ks[12], (DEPTH, D_AAA_LORA, C_RWKV), 0.5 * D_AAA_LORA ** -0.5),
        'k_k': 0.85 + nrm(ks[13], (DEPTH, C_RWKV), 0.02),
        'k_a': 1.0 + nrm(ks[14], (DEPTH, C_RWKV), 0.02),
        'r_k': nrm(ks[15], (DEPTH, H_RWKV, RWKV_HEAD), 0.1),
        'lnx_w': 1.0 + nrm(ks[16], (DEPTH, C_RWKV), 0.02),
        'lnx_b': nrm(ks[17], (DEPTH, C_RWKV), 0.02),
        'lam_q1': nrm(ks[18], (DEPTH, DK_DIFF), 0.1),
        'lam_k1': nrm(ks[19], (DEPTH, DK_DIFF), 0.1),
        'lam_q2': nrm(ks[20], (DEPTH, DK_DIFF), 0.1),
        'lam_k2': nrm(ks[21], (DEPTH, DK_DIFF), 0.1),
        'subln_g': 1.0 + nrm(ks[22], (DEPTH, DV_DIFF), 0.02),
        'w_out': nrm(ks[23], (DEPTH, D_MIX, D_MODEL), 0.5 * D_MIX ** -0.5),
        'final_g': 1.0 + nrm(ks[24], (D_MODEL,), 0.02),
    }


def reference(x_prompt, x_sample, cache_k, cache_v, state_wkv, state_shift, norm_g, w_in, shift_mu,
              w_decay0, w_decay2, a0, w_a2, k_k, k_a, r_k, lnx_w, lnx_b, lam_q1, lam_k1, lam_q2, lam_k2,
              subln_g, w_out, final_g):
    xp, xs = x_prompt, x_sample
    bp = xp.shape[0]
    kp_l, vp_l, sp_l, hp_l = [], [], [], []
    ks_l, vs_l, ss_l, hs_l = [], [], [], []
    for l in range(DEPTH):
        p = (norm_g[l], w_in[l], shift_mu[l], w_decay0[l], w_decay2[l], a0[l], w_a2[l], k_k[l], k_a[l],
             r_k[l], lnx_w[l], lnx_b[l], lam_q1[l], lam_k1[l], lam_q2[l], lam_k2[l], subln_g[l], w_out[l])
        empty_k = jnp.zeros((bp, 0, H_DIFF, 2 * DK_DIFF), xp.dtype)
        empty_v = jnp.zeros((bp, 0, H_DIFF, DV_DIFF), xp.dtype)
        wkv0 = jnp.zeros((bp, H_RWKV, RWKV_HEAD, RWKV_HEAD), state_wkv.dtype)
        prev0 = jnp.zeros((bp, D_SHIFT), state_shift.dtype)
        xp, kn, vn, sn, hn = mixer_layer(xp, l, empty_k, empty_v, wkv0, prev0, *p)
        kp_l.append(kn); vp_l.append(vn); sp_l.append(sn); hp_l.append(hn)
        xs, kn, vn, sn, hn = mixer_layer(xs, l, cache_k[l], cache_v[l], state_wkv[l], state_shift[l], *p)
        ks_l.append(kn); vs_l.append(vn); ss_l.append(sn); hs_l.append(hn)
    return (rmsnorm(xp, final_g), rmsnorm(xs, final_g),
            jnp.stack(kp_l), jnp.stack(vp_l), jnp.stack(sp_l), jnp.stack(hp_l),
            jnp.stack(ks_l), jnp.stack(vs_l), jnp.stack(ss_l), jnp.stack(hs_l))
```

```python
import functools
import math

import jax
import jax.numpy as jnp
from jax import lax
from jax.experimental import pallas as pl
from jax.experimental.pallas import tpu as pltpu

F32 = jnp.float32
BF16 = jnp.bfloat16

D_MODEL = 2048
CHUNK = 64
C_RWKV = 1024
RWKV_HEAD = 64
H_RWKV = C_RWKV // RWKV_HEAD
N_PAIR = H_RWKV // 2
D_LORA = 64
D_SHIFT = 3 * C_RWKV + 2 * D_LORA
C_DIFF = 1024
H_DIFF = 8
DV_DIFF = 128
DK_DIFF = 64
D_REST = C_RWKV + 4 * C_DIFF
NORM_EPS = 1e-6
RWKV_LN_EPS = 64e-5
SUBLN_EPS = 1e-5
NEG_INF = -1e30
LANES = 128
L_CHUNK = 64
EXP_M05 = math.exp(-0.5)
VMEM_LIMIT = 48 * 1024 * 1024


def _lambda_init(l):
    return 0.8 - 0.6 * math.exp(-0.3 * l)


def _cparams(sem):
    return pltpu.CompilerParams(dimension_semantics=sem, vmem_limit_bytes=VMEM_LIMIT)


_NN = (((1,), (0,)), ((), ()))
_NT = (((1,), (1,)), ((), ()))
_TN = (((0,), (0,)), ((), ()))


def _dot(a, b, dims=_NN):
    return lax.dot_general(a, b, dims, preferred_element_type=F32)


def _split2(x):
    hi = x.astype(BF16)
    lo = (x - hi.astype(F32)).astype(BF16)
    return hi, lo


def _split3(x):
    hi = x.astype(BF16)
    r1 = x - hi.astype(F32)
    mid = r1.astype(BF16)
    lo = (r1 - mid.astype(F32)).astype(BF16)
    return hi, mid, lo


def _mm3(a, b, dims=_NN):
    ah, al = _split2(a)
    bh, bl = _split2(b)
    return _dot(ah, bh, dims) + (_dot(ah, bl, dims) + _dot(al, bh, dims))


def _mm_exact_lhs(a_bf16, b, dims=_NN):
    bh, bm, bl = _split3(b)
    return _dot(a_bf16, bh, dims) + (_dot(a_bf16, bm, dims) + _dot(a_bf16, bl, dims))


def _mm_exact_rhs(a, b_bf16):
    ah, am, al = _split3(a)
    return _dot(ah, b_bf16) + (_dot(am, b_bf16) + _dot(al, b_bf16))


def _sigmoid(x):
    return 1.0 / (1.0 + jnp.exp(-x))


def _norm_matmul_kernel(x_ref, g_ref, w_ref, o_ref, h_ref):
    @pl.when(pl.program_id(1) == 0)
    def _():
        x = x_ref[...]
        ms = jnp.mean(x * x, axis=-1, keepdims=True)
        h_ref[...] = (x * lax.rsqrt(ms + NORM_EPS) * g_ref[...]).astype(BF16)

    o_ref[...] = jnp.dot(h_ref[...], w_ref[...], preferred_element_type=F32)


def _norm_matmul(x, g, w, tm, tn):
    m, d = x.shape
    n = w.shape[1]
    return pl.pallas_call(
        _norm_matmul_kernel,
        grid=(m // tm, n // tn),
        in_specs=[
            pl.BlockSpec((tm, d), lambda i, j: (i, 0)),
            pl.BlockSpec((1, d), lambda i, j: (0, 0)),
            pl.BlockSpec((d, tn), lambda i, j: (0, j)),
        ],
        out_specs=pl.BlockSpec((tm, tn), lambda i, j: (i, j)),
        out_shape=jax.ShapeDtypeStruct((m, n), F32),
        scratch_shapes=[pltpu.VMEM((tm, d), BF16)],
        compiler_params=_cparams(("parallel", "arbitrary")),
        name="norm_matmul",
    )(x, g, w)


def _final_norm_kernel(x_ref, g_ref, o_ref):
    x = x_ref[...]
    ms = jnp.mean(x * x, axis=-1, keepdims=True)
    o_ref[...] = x * lax.rsqrt(ms + NORM_EPS) * g_ref[...]


def _final_norm(x, g, tm):
    m, d = x.shape
    return pl.pallas_call(
        _final_norm_kernel,
        grid=(m // tm,),
        in_specs=[pl.BlockSpec((tm, d), lambda i: (i, 0)), pl.BlockSpec((1, d), lambda i: (0, 0))],
        out_specs=pl.BlockSpec((tm, d), lambda i: (i, 0)),
        out_shape=jax.ShapeDtypeStruct((m, d), F32),
        compiler_params=_cparams(("parallel",)),
        name="final_norm",
    )(x, g)


def _out_matmul_kernel(x_ref, yr_ref, yd_ref, wr_ref, wd_ref, o_ref):
    acc = jnp.dot(yr_ref[...], wr_ref[...], preferred_element_type=F32)
    acc = acc + jnp.dot(yd_ref[...], wd_ref[...], preferred_element_type=F32)
    o_ref[...] = x_ref[...] + acc


def _out_matmul(x, y_r, y_d, w_out, tm, tn):
    m, d = x.shape
    return pl.pallas_call(
        _out_matmul_kernel,
        grid=(m // tm, d // tn),
        in_specs=[
            pl.BlockSpec((tm, tn), lambda i, j: (i, j)),
            pl.BlockSpec((tm, C_RWKV), lambda i, j: (i, 0)),
            pl.BlockSpec((tm, C_DIFF), lambda i, j: (i, 0)),
            pl.BlockSpec((C_RWKV, tn), lambda i, j: (0, j)),
            pl.BlockSpec((C_DIFF, tn), lambda i, j: (1, j)),
        ],
        out_specs=pl.BlockSpec((tm, tn), lambda i, j: (i, j)),
        out_shape=jax.ShapeDtypeStruct((m, d), F32),
        compiler_params=_cparams(("parallel", "parallel")),
        name="out_matmul",
    )(x, y_r, y_d, w_out, w_out)


def _rwkv_kernel(sh_ref, gr_ref, s0_ref, p0_ref, mu_ref, wd0_ref, wd2_ref, a0_ref, wa2_ref,
                 kk_ref, ka_ref, rk_ref, lnw_ref, lnb_ref,
                 y_ref, st_ref,
                 bd_ref, prev_ref, rt_ref, kt_ref, ki_ref, bi_ref, kh_ref, bh_ref, v_ref, rkw_ref, gl_ref):
    c = pl.program_id(1)
    nc = pl.num_programs(1)
    L = L_CHUNK
    C = C_RWKV

    @pl.when(c == 0)
    def _():
        prev_ref[...] = p0_ref[0]
        for p in range(N_PAIR):
            bd_ref[p] = jnp.zeros((LANES, LANES), F32)
            bd_ref[p, 0:RWKV_HEAD, 0:RWKV_HEAD] = s0_ref[0, 2 * p]
            bd_ref[p, RWKV_HEAD:LANES, RWKV_HEAD:LANES] = s0_ref[0, 2 * p + 1]

    sh = sh_ref[...]
    row0 = lax.broadcasted_iota(jnp.int32, sh.shape, 0) == 0
    prev = jnp.where(row0, prev_ref[...], pltpu.roll(sh, 1, 0))
    xs = sh + (prev - sh) * mu_ref[...]
    prev_ref[...] = sh_ref[L - 1:L, :]

    r = xs[:, 0:C]
    kr = xs[:, C:2 * C]
    v = xs[:, 2 * C:3 * C]
    la = xs[:, 3 * C:3 * C + LANES]

    lane = lax.broadcasted_iota(jnp.int32, (L, LANES), 1)
    head0 = lane < RWKV_HEAD
    w2 = jnp.concatenate([wd2_ref[...], wa2_ref[...]], axis=0)
    dw = _mm3(jnp.where(head0, jnp.tanh(la), 0.0), w2)
    da = _mm3(jnp.where(head0, 0.0, la), w2)
    logw = -EXP_M05 * _sigmoid(wd0_ref[...] + dw)
    a = _sigmoid(a0_ref[...] + da)

    er = lax.broadcasted_iota(jnp.int32, (LANES, LANES), 0)
    ec = lax.broadcasted_iota(jnp.int32, (LANES, LANES), 1)
    e2 = jnp.where((er < RWKV_HEAD) == (ec < RWKV_HEAD), 1.0, 0.0).astype(BF16)

    kk = kr * kk_ref[...]
    kk2 = kk * kk
    ss = jnp.concatenate(
        [_mm_exact_rhs(kk2[:, p * LANES:(p + 1) * LANES], e2) for p in range(N_PAIR)], axis=1)
    kk = kk / jnp.maximum(jnp.sqrt(ss), 1e-12)
    kw = kr * (1.0 + (a - 1.0) * ka_ref[...])
    b = kk * a

    tr = lax.broadcasted_iota(jnp.int32, (L, L), 0)
    tc = lax.broadcasted_iota(jnp.int32, (L, L), 1)
    tril = jnp.where(tr >= tc, 1.0, 0.0).astype(BF16)
    cum = _mm_exact_lhs(tril, logw)
    cum_l = cum[L - 1:L, :]
    gi = jnp.exp(-cum)
    gl = jnp.exp(cum_l - cum)
    rt_ref[...] = r * jnp.exp(cum)
    kt_ref[...] = kk * jnp.exp(cum - logw)
    ki_ref[...] = kw * gi
    bi_ref[...] = b * gi
    kh_ref[...] = kw * gl
    bh_ref[...] = b * gl
    v_ref[...] = v
    rkw_ref[...] = r * kw * rk_ref[...]
    gl_ref[...] = jnp.broadcast_to(jnp.exp(cum_l), gl_ref.shape)

    strict = er > ec
    incl = er >= ec
    eye = jnp.where(er == ec, 1.0, 0.0)

    def blockdiag(x):
        return jnp.concatenate([jnp.where(head0, x, 0.0), jnp.where(head0, 0.0, x)], axis=0)

    def pair_body(p, carry):
        sl = pl.ds(pl.multiple_of(p * LANES, LANES), LANES)
        kt = blockdiag(kt_ref[:, sl])
        rt = blockdiag(rt_ref[:, sl])
        ki = blockdiag(ki_ref[:, sl])
        bi = blockdiag(bi_ref[:, sl])
        kh = blockdiag(kh_ref[:, sl])
        bh = blockdiag(bh_ref[:, sl])
        vp = v_ref[:, sl]
        vb = blockdiag(vp)
        lhs = jnp.concatenate([kt, rt], axis=0)
        ak = _mm3(lhs, ki, _NT)
        ab = _mm3(lhs, bi, _NT)
        a_kk = jnp.where(strict, ak[:2 * L], 0.0)
        a_rk = jnp.where(incl, ak[2 * L:], 0.0)
        a_kb = jnp.where(strict, ab[:2 * L], 0.0)
        a_rb = jnp.where(incl, ab[2 * L:], 0.0)
        q = -a_kb
        t = eye + q
        for _ in range(5):
            q = _mm3(q, q)
            t = t + _mm3(t, q)
        s0 = bd_ref[p]
        zy = _mm3(lhs, s0, _NT)
        z = zy[:2 * L] + _mm3(a_kk, vb)
        u = _mm3(t, z)
        vu = jnp.concatenate([vb, u], axis=0)
        yb = zy[2 * L:] + _mm3(jnp.concatenate([a_rk, -a_rb], axis=1), vu)
        bd_ref[p] = s0 * gl_ref[0:1, sl] + _mm3(vu, jnp.concatenate([kh, -bh], axis=0), _TN)
        y = yb[:L] + yb[L:]
        mean = _mm_exact_rhs(y, e2) * (1.0 / RWKV_HEAD)
        d = y - mean
        var = _mm_exact_rhs(d * d, e2) * (1.0 / RWKV_HEAD)
        yn = d * lax.rsqrt(var + RWKV_LN_EPS) * lnw_ref[:, sl] + lnb_ref[:, sl]
        bonus = _mm_exact_rhs(rkw_ref[:, sl], e2) * vp
        g = gr_ref[:, sl]
        y_ref[:, sl] = ((yn + bonus) * (g * _sigmoid(g))).astype(y_ref.dtype)
        return carry

    lax.fori_loop(0, N_PAIR, pair_body, 0, unroll=2)

    @pl.when(c == nc - 1)
    def _():
        for p in range(N_PAIR):
            st_ref[0, 2 * p] = bd_ref[p, 0:RWKV_HEAD, 0:RWKV_HEAD]
            st_ref[0, 2 * p + 1] = bd_ref[p, RWKV_HEAD:LANES, RWKV_HEAD:LANES]


def _rwkv_mix(sh, rest, s0, prev0, mu, wd0, wd2, a0, wa2, k_k, k_a, r_k, lnw, lnb):
    bsz = s0.shape[0]
    t = sh.shape[0] // bsz
    assert t % L_CHUNK == 0
    nc = t // L_CHUNK
    L = L_CHUNK
    row = lambda b, c: (b * nc + c, 0)
    vec = lambda n: pl.BlockSpec((1, n), lambda b, c: (0, 0))
    full = lambda shape: pltpu.VMEM(shape, F32)
    return pl.pallas_call(
        _rwkv_kernel,
        grid=(bsz, nc),
        in_specs=[
            pl.BlockSpec((L, D_SHIFT), row),
            pl.BlockSpec((L, C_RWKV), row),
            pl.BlockSpec((1, H_RWKV, RWKV_HEAD, RWKV_HEAD), lambda b, c: (b, 0, 0, 0)),
            pl.BlockSpec((1, 1, D_SHIFT), lambda b, c: (b, 0, 0)),
            vec(D_SHIFT), vec(C_RWKV),
            pl.BlockSpec((D_LORA, C_RWKV), lambda b, c: (0, 0)),
            vec(C_RWKV),
            pl.BlockSpec((D_LORA, C_RWKV), lambda b, c: (0, 0)),
            vec(C_RWKV), vec(C_RWKV), vec(C_RWKV), vec(C_RWKV), vec(C_RWKV),
        ],
        out_specs=[
            pl.BlockSpec((L, C_RWKV), row),
            pl.BlockSpec((1, H_RWKV, RWKV_HEAD, RWKV_HEAD), lambda b, c: (b, 0, 0, 0)),
        ],
        out_shape=[
            jax.ShapeDtypeStruct((bsz * t, C_RWKV), BF16),
            jax.ShapeDtypeStruct((bsz, H_RWKV, RWKV_HEAD, RWKV_HEAD), F32),
        ],
        scratch_shapes=[
            full((N_PAIR, LANES, LANES)), full((1, D_SHIFT)),
            full((L, C_RWKV)), full((L, C_RWKV)), full((L, C_RWKV)), full((L, C_RWKV)),
            full((L, C_RWKV)), full((L, C_RWKV)), full((L, C_RWKV)), full((L, C_RWKV)),
            full((8, C_RWKV)),
        ],
        compiler_params=_cparams(("parallel", "arbitrary")),
        name="rwkv_mix",
    )(sh, rest, s0, prev0, mu, wd0, wd2, a0, wa2, k_k, k_a, r_k, lnw, lnb)


def _stack_maps(q):
    lane = lax.broadcasted_iota(jnp.int32, q.shape, 1)
    m0 = lane < DK_DIFF
    return jnp.concatenate([jnp.where(m0, q, 0.0), jnp.where(m0, 0.0, q)], axis=0).astype(BF16)


def _softmax_step(s, v, m_ref, l_ref, acc_ref, idx):
    m_old = m_ref[idx]
    m_new = jnp.maximum(m_old, jnp.max(s, axis=-1, keepdims=True))
    alpha = jnp.exp(m_old - m_new)
    p = jnp.exp(s - m_new)
    l_ref[idx] = alpha * l_ref[idx] + jnp.sum(p, axis=-1, keepdims=True)
    acc_ref[idx] = alpha * acc_ref[idx] + jnp.dot(p.astype(BF16), v, preferred_element_type=F32)
    m_ref[idx] = m_new


def _diff_finish(acc, l, lam, lam_init, subln_g, gate):
    t = acc.shape[0] // 2
    o = acc[:t] / l[:t] - lam * (acc[t:] / l[t:])
    o = o * lax.rsqrt(jnp.mean(o * o, axis=-1, keepdims=True) + SUBLN_EPS) * subln_g * (1.0 - lam_init)
    return o * (gate * _sigmoid(gate))


def _lambda(lq1_ref, lk1_ref, lq2_ref, lk2_ref, lam_init):
    s1 = jnp.sum(lq1_ref[...] * lk1_ref[...], axis=-1, keepdims=True)
    s2 = jnp.sum(lq2_ref[...] * lk2_ref[...], axis=-1, keepdims=True)
    return jnp.exp(s1) - jnp.exp(s2) + lam_init


def _local_bias(slope, nq2, nk, tq):
    i = lax.broadcasted_iota(jnp.int32, (nq2, nk), 0)
    i = jnp.where(i >= tq, i - tq, i)
    j = lax.broadcasted_iota(jnp.int32, (nq2, nk), 1)
    bias = slope * (i - jnp.abs(i - j)).astype(F32)
    shift = CHUNK.bit_length() - 1
    visible = jnp.right_shift(j, shift) <= jnp.right_shift(i, shift)
    return bias, visible


def _attn_prompt_kernel(q_ref, k_ref, v_ref, gd_ref, lq1_ref, lk1_ref, lq2_ref, lk2_ref, sg_ref,
                        o_ref, m_ref, l_ref, acc_ref, *, tq, tk, lam_init):
    h = pl.program_id(0)
    qi = pl.program_id(1)
    kj = pl.program_id(2)
    slope = lax.bitcast_convert_type(jnp.full((1, 1), (126 - h) << 23, jnp.int32), F32)

    @pl.when(kj == 0)
    def _():
        m_ref[0] = jnp.full(m_ref.shape[1:], NEG_INF, F32)
        l_ref[0] = jnp.zeros(l_ref.shape[1:], F32)
        acc_ref[0] = jnp.zeros(acc_ref.shape[1:], F32)

    def scores():
        q2 = _stack_maps(q_ref[...] * (DK_DIFF ** -0.5))
        return _dot(q2, k_ref[...].astype(BF16), _NT)

    @pl.when(kj < qi)
    def _():
        j = lax.broadcasted_iota(jnp.int32, (1, tk), 1)
        bias = slope * (j - (qi - kj) * tq).astype(F32)
        _softmax_step(scores() + bias, v_ref[...].astype(BF16), m_ref, l_ref, acc_ref, 0)

    @pl.when(kj == qi)
    def _():
        bias, visible = _local_bias(slope, 2 * tq, tk, tq)
        s = jnp.where(visible, scores() + bias, NEG_INF)
        _softmax_step(s, v_ref[...].astype(BF16), m_ref, l_ref, acc_ref, 0)
        lam = _lambda(lq1_ref, lk1_ref, lq2_ref, lk2_ref, lam_init)
        o_ref[...] = _diff_finish(acc_ref[0], l_ref[0], lam, lam_init, sg_ref[...], gd_ref[...]).astype(o_ref.dtype)


def _attn_prompt(rest, lq1, lk1, lq2, lk2, subln_g, lam_init, tq):
    t = rest.shape[0]
    assert t % tq == 0 and tq % CHUNK == 0
    nq = t // tq
    qcol, kcol, vcol, gcol = (C_RWKV // DV_DIFF + n * H_DIFF for n in range(4))
    vec = lambda n: pl.BlockSpec((1, n), lambda h, i, j: (0, 0))
    kern = functools.partial(_attn_prompt_kernel, tq=tq, tk=tq, lam_init=lam_init)
    return pl.pallas_call(
        kern,
        grid=(H_DIFF, nq, nq),
        in_specs=[
            pl.BlockSpec((tq, DV_DIFF), lambda h, i, j: (i, qcol + h)),
            pl.BlockSpec((tq, DV_DIFF), lambda h, i, j: (jnp.minimum(j, i), kcol + h)),
            pl.BlockSpec((tq, DV_DIFF), lambda h, i, j: (jnp.minimum(j, i), vcol + h)),
            pl.BlockSpec((tq, DV_DIFF), lambda h, i, j: (i, gcol + h)),
            vec(DK_DIFF), vec(DK_DIFF), vec(DK_DIFF), vec(DK_DIFF), vec(DV_DIFF),
        ],
        out_specs=pl.BlockSpec((tq, DV_DIFF), lambda h, i, j: (i, h)),
        out_shape=jax.ShapeDtypeStruct((t, C_DIFF), BF16),
        scratch_shapes=[
            pltpu.VMEM((1, 2 * tq, 1), F32), pltpu.VMEM((1, 2 * tq, 1), F32),
            pltpu.VMEM((1, 2 * tq, DV_DIFF), F32),
        ],
        compiler_params=_cparams(("parallel", "parallel", "arbitrary")),
        name="attn_prompt",
    )(rest, rest, rest, rest, lq1, lk1, lq2, lk2, subln_g)


def _attn_sample_kernel(q_ref, kn_ref, vn_ref, gd_ref, ck_ref, cv_ref, lq1_ref, lk1_ref, lq2_ref, lk2_ref, sg_ref,
                        o_ref, m_ref, l_ref, acc_ref, *, t, tk, past_len, lam_init):
    kj = pl.program_id(1)
    nk = pl.num_programs(1) - 1

    @pl.when(kj == 0)
    def _():
        m_ref[...] = jnp.full(m_ref.shape, NEG_INF, F32)
        l_ref[...] = jnp.zeros(l_ref.shape, F32)
        acc_ref[...] = jnp.zeros(acc_ref.shape, F32)

    def head_q(h):
        return _stack_maps(q_ref[:, h * DV_DIFF:(h + 1) * DV_DIFF] * (DK_DIFF ** -0.5))

    @pl.when(kj < nk)
    def _():
        j = lax.broadcasted_iota(jnp.int32, (1, tk), 1)
        dist = (j + kj * tk - past_len).astype(F32)
        for h in range(H_DIFF):
            cols = slice(h * DV_DIFF, (h + 1) * DV_DIFF)
            s = _dot(head_q(h), ck_ref[:, cols].astype(BF16), _NT) + (2.0 ** -(h + 1)) * dist
            _softmax_step(s, cv_ref[:, cols].astype(BF16), m_ref, l_ref, acc_ref, h)

    @pl.when(kj == nk)
    def _():
        lam = _lambda(lq1_ref, lk1_ref, lq2_ref, lk2_ref, lam_init)
        for h in range(H_DIFF):
            cols = slice(h * DV_DIFF, (h + 1) * DV_DIFF)
            bias, visible = _local_bias(2.0 ** -(h + 1), 2 * t, t, t)
            s = _dot(head_q(h), kn_ref[:, cols].astype(BF16), _NT) + bias
            _softmax_step(jnp.where(visible, s, NEG_INF), vn_ref[:, cols].astype(BF16), m_ref, l_ref, acc_ref, h)
            o_ref[:, cols] = _diff_finish(acc_ref[h], l_ref[h], lam, lam_init, sg_ref[...],
                                          gd_ref[:, cols]).astype(o_ref.dtype)


def _attn_sample(rest, cache_k, cache_v, layer, lq1, lk1, lq2, lk2, subln_g, lam_init, tk):
    bsz, past_len = cache_k.shape[1], cache_k.shape[2]
    t = rest.shape[0] // bsz
    assert t <= CHUNK and past_len % CHUNK == 0 and past_len % tk == 0
    nk = past_len // tk
    vec = lambda n: pl.BlockSpec((1, n), lambda b, j: (0, 0))
    new = lambda col: pl.BlockSpec((t, C_DIFF), lambda b, j: (b, col))
    past = pl.BlockSpec((None, None, tk, C_DIFF), lambda b, j: (layer, b, jnp.minimum(j, nk - 1), 0))
    kern = functools.partial(_attn_sample_kernel, t=t, tk=tk, past_len=past_len, lam_init=lam_init)
    return pl.pallas_call(
        kern,
        grid=(bsz, nk + 1),
        in_specs=[new(1), new(2), new(3), new(4), past, past,
                  vec(DK_DIFF), vec(DK_DIFF), vec(DK_DIFF), vec(DK_DIFF), vec(DV_DIFF)],
        out_specs=pl.BlockSpec((t, C_DIFF), lambda b, j: (b, 0)),
        out_shape=jax.ShapeDtypeStruct((bsz * t, C_DIFF), BF16),
        scratch_shapes=[
            pltpu.VMEM((H_DIFF, 2 * t, 1), F32), pltpu.VMEM((H_DIFF, 2 * t, 1), F32),
            pltpu.VMEM((H_DIFF, 2 * t, DV_DIFF), F32),
        ],
        compiler_params=_cparams(("parallel", "arbitrary")),
        name="attn_sample",
    )(rest, rest, rest, rest, cache_k, cache_v, lq1, lk1, lq2, lk2, subln_g)


def kernel(x_prompt, x_sample, cache_k, cache_v, state_wkv, state_shift, norm_g, w_in, shift_mu, w_decay0, w_decay2,
           a0, w_a2, k_k, k_a, r_k, lnx_w, lnx_b, lam_q1, lam_k1, lam_q2, lam_k2, subln_g, w_out, final_g):
    depth = w_in.shape[0]
    bp, tp, d = x_prompt.shape
    bs, ts, _ = x_sample.shape
    assert bp == 1, "the prompt attention kernel handles one sequence"
    past_len = cache_k.shape[2]

    xp = x_prompt.reshape(bp * tp, d)
    xs = x_sample.reshape(bs * ts, d)
    ck = cache_k.reshape(depth, bs, past_len, C_DIFF)
    cv = cache_v.reshape(depth, bs, past_len, C_DIFF)
    w_in_b = w_in.astype(BF16)
    w_out_b = w_out.astype(BF16)
    zero_state = jnp.zeros((bp, H_RWKV, RWKV_HEAD, RWKV_HEAD), state_wkv.dtype)
    zero_shift = jnp.zeros((bp, 1, D_SHIFT), state_shift.dtype)
    row = lambda a: a.reshape(1, -1)

    outs = {name: [] for name in ("kp", "vp", "sp", "hp", "ks", "vs", "ss", "hs")}
    for l in range(depth):
        lam_init = _lambda_init(l)
        g = row(norm_g[l])
        w_sh = w_in_b[l, :, :D_SHIFT]
        w_rest = w_in_b[l, :, D_SHIFT:]
        rw = (row(shift_mu[l]), row(w_decay0[l]), w_decay2[l], row(a0[l]), w_a2[l], row(k_k[l]), row(k_a[l]),
              row(r_k[l]), row(lnx_w[l]), row(lnx_b[l]))
        lam = (row(lam_q1[l]), row(lam_k1[l]), row(lam_q2[l]), row(lam_k2[l]), row(subln_g[l]))

        def stream(x, bsz, s0, prev0, attn):
            sh = _norm_matmul(x, g, w_sh, 512, 640)
            rest = _norm_matmul(x, g, w_rest, 512, 1024)
            y_r, s_t = _rwkv_mix(sh, rest, s0, prev0, *rw)
            y_d = attn(rest)
            x = _out_matmul(x, y_r, y_d, w_out_b[l], 512, 1024)
            t = sh.shape[0] // bsz
            k_new = rest[:, C_RWKV + C_DIFF:C_RWKV + 2 * C_DIFF].reshape(bsz, t, H_DIFF, 2 * DK_DIFF)
            v_new = rest[:, C_RWKV + 2 * C_DIFF:C_RWKV + 3 * C_DIFF].reshape(bsz, t, H_DIFF, DV_DIFF)
            shift = sh.reshape(bsz, t, D_SHIFT)[:, -1]
            return x, k_new, v_new, s_t, shift

        xp, kn, vn, sn, hn = stream(
            xp, bp, zero_state, zero_shift,
            lambda rest: _attn_prompt(rest, *lam, lam_init, 512))
        outs["kp"].append(kn); outs["vp"].append(vn); outs["sp"].append(sn); outs["hp"].append(hn)
        xs, kn, vn, sn, hn = stream(
            xs, bs, state_wkv[l], state_shift[l].reshape(bs, 1, D_SHIFT),
            lambda rest: _attn_sample(rest, ck, cv, l, *lam, lam_init, 1024))
        outs["ks"].append(kn); outs["vs"].append(vn); outs["ss"].append(sn); outs["hs"].append(hn)

    yp = _final_norm(xp, row(final_g), 512).reshape(bp, tp, d)
    ys = _final_norm(xs, row(final_g), 512).reshape(bs, ts, d)
    st = lambda name: jnp.stack(outs[name])
    return (yp, ys, st("kp"), st("vp"), st("sp"), st("hp"), st("ks"), st("vs"), st("ss"), st("hs"))
```

```python
import functools
import math

import jax
import jax.numpy as jnp
from jax import lax
from jax.experimental import pallas as pl
from jax.experimental.pallas import tpu as pltpu

F32 = jnp.float32
BF16 = jnp.bfloat16

D_MODEL = 2048
CHUNK = 64
C_RWKV = 1024
RWKV_HEAD = 64
H_RWKV = C_RWKV // RWKV_HEAD
N_PAIR = H_RWKV // 2
D_LORA = 64
D_SHIFT = 3 * C_RWKV + 2 * D_LORA
C_DIFF = 1024
H_DIFF = 8
DV_DIFF = 128
DK_DIFF = 64
D_REST = C_RWKV + 4 * C_DIFF
NORM_EPS = 1e-6
RWKV_LN_EPS = 64e-5
SUBLN_EPS = 1e-5
NEG_INF = -1e30
LANES = 128
L_CHUNK = 64
EXP_M05 = math.exp(-0.5)
LOG2E = math.log2(math.e)
ROW_GROUP = 16
VMEM_LIMIT = 48 * 1024 * 1024


def _lambda_init(l):
    return 0.8 - 0.6 * math.exp(-0.3 * l)


def _cparams(sem):
    return pltpu.CompilerParams(dimension_semantics=sem, vmem_limit_bytes=VMEM_LIMIT)


_NN = (((1,), (0,)), ((), ()))
_NT = (((1,), (1,)), ((), ()))
_TN = (((0,), (0,)), ((), ()))
_BNN = (((2,), (1,)), ((0,), (0,)))
_BNT = (((2,), (2,)), ((0,), (0,)))
_BTN = (((1,), (1,)), ((0,), (0,)))


def _dot(a, b, dims=_NN):
    return lax.dot_general(a, b, dims, preferred_element_type=F32)


def _split2(x):
    hi = x.astype(BF16)
    lo = (x - hi.astype(F32)).astype(BF16)
    return hi, lo


def _split3(x):
    hi = x.astype(BF16)
    r1 = x - hi.astype(F32)
    mid = r1.astype(BF16)
    lo = (r1 - mid.astype(F32)).astype(BF16)
    return hi, mid, lo


def _mm3(a, b, dims=_NN):
    ah, al = _split2(a)
    bh, bl = _split2(b)
    return _dot(ah, bh, dims) + (_dot(ah, bl, dims) + _dot(al, bh, dims))


def _mm_exact_lhs(a_bf16, b, dims=_NN):
    bh, bm, bl = _split3(b)
    return _dot(a_bf16, bh, dims) + (_dot(a_bf16, bm, dims) + _dot(a_bf16, bl, dims))


def _sum_heads(a, e2):
    return _dot(a.astype(BF16), e2)


def _sigmoid(x):
    return 1.0 / (1.0 + jnp.exp(-x))


def _norm_matmul_kernel(x_ref, g_ref, w_ref, o_ref, h_ref):
    @pl.when(pl.program_id(1) == 0)
    def _():
        x = x_ref[...]
        ms = jnp.mean(x * x, axis=-1, keepdims=True)
        h_ref[...] = (x * lax.rsqrt(ms + NORM_EPS) * g_ref[...]).astype(BF16)

    o_ref[...] = jnp.dot(h_ref[...], w_ref[...], preferred_element_type=F32)


def _norm_matmul(x, g, w, tm, tn):
    m, d = x.shape
    n = w.shape[1]
    return pl.pallas_call(
        _norm_matmul_kernel,
        grid=(m // tm, n // tn),
        in_specs=[
            pl.BlockSpec((tm, d), lambda i, j: (i, 0)),
            pl.BlockSpec((1, d), lambda i, j: (0, 0)),
            pl.BlockSpec((d, tn), lambda i, j: (0, j)),
        ],
        out_specs=pl.BlockSpec((tm, tn), lambda i, j: (i, j)),
        out_shape=jax.ShapeDtypeStruct((m, n), F32),
        scratch_shapes=[pltpu.VMEM((tm, d), BF16)],
        compiler_params=_cparams(("parallel", "arbitrary")),
        name="norm_matmul",
    )(x, g, w)


def _final_norm_kernel(x_ref, g_ref, o_ref):
    x = x_ref[...]
    ms = jnp.mean(x * x, axis=-1, keepdims=True)
    o_ref[...] = x * lax.rsqrt(ms + NORM_EPS) * g_ref[...]


def _final_norm(x, g, tm):
    m, d = x.shape
    return pl.pallas_call(
        _final_norm_kernel,
        grid=(m // tm,),
        in_specs=[pl.BlockSpec((tm, d), lambda i: (i, 0)), pl.BlockSpec((1, d), lambda i: (0, 0))],
        out_specs=pl.BlockSpec((tm, d), lambda i: (i, 0)),
        out_shape=jax.ShapeDtypeStruct((m, d), F32),
        compiler_params=_cparams(("parallel",)),
        name="final_norm",
    )(x, g)


def _out_matmul_kernel(x_ref, yr_ref, yd_ref, wr_ref, wd_ref, o_ref):
    acc = jnp.dot(yr_ref[...], wr_ref[...], preferred_element_type=F32)
    acc = acc + jnp.dot(yd_ref[...], wd_ref[...], preferred_element_type=F32)
    o_ref[...] = x_ref[...] + acc


def _out_matmul(x, y_r, y_d, w_out, tm, tn):
    m, d = x.shape
    return pl.pallas_call(
        _out_matmul_kernel,
        grid=(m // tm, d // tn),
        in_specs=[
            pl.BlockSpec((tm, tn), lambda i, j: (i, j)),
            pl.BlockSpec((tm, C_RWKV), lambda i, j: (i, 0)),
            pl.BlockSpec((tm, C_DIFF), lambda i, j: (i, 0)),
            pl.BlockSpec((C_RWKV, tn), lambda i, j: (0, j)),
            pl.BlockSpec((C_DIFF, tn), lambda i, j: (1, j)),
        ],
        out_specs=pl.BlockSpec((tm, tn), lambda i, j: (i, j)),
        out_shape=jax.ShapeDtypeStruct((m, d), F32),
        compiler_params=_cparams(("parallel", "parallel")),
        name="out_matmul",
    )(x, y_r, y_d, w_out, w_out)


def _rwkv_kernel(sh_ref, gr_ref, s0_ref, p0_ref, mu_ref, wd0_ref, wd2_ref, a0_ref, wa2_ref,
                 kk_ref, ka_ref, rk_ref, lnw_ref, lnb_ref,
                 y_ref, st_ref,
                 bd_ref, prev_ref):
    c = pl.program_id(1)
    nc = pl.num_programs(1)
    L = L_CHUNK
    C = C_RWKV

    @pl.when(c == 0)
    def _():
        prev_ref[...] = p0_ref[0]
        for p in range(N_PAIR):
            bd_ref[p] = jnp.zeros((LANES, LANES), F32)
            bd_ref[p, 0:RWKV_HEAD, 0:RWKV_HEAD] = s0_ref[0, 2 * p]
            bd_ref[p, RWKV_HEAD:LANES, RWKV_HEAD:LANES] = s0_ref[0, 2 * p + 1]

    sh = sh_ref[...]
    row0 = lax.broadcasted_iota(jnp.int32, sh.shape, 0) == 0
    prev = jnp.where(row0, prev_ref[...], pltpu.roll(sh, 1, 0))
    xs = sh + (prev - sh) * mu_ref[...]
    prev_ref[...] = sh_ref[L - 1:L, :]

    r = xs[:, 0:C]
    kr = xs[:, C:2 * C]
    v = xs[:, 2 * C:3 * C]
    la = xs[:, 3 * C:3 * C + LANES]

    lane = lax.broadcasted_iota(jnp.int32, (L, LANES), 1)
    head0 = lane < RWKV_HEAD
    w2 = jnp.concatenate([wd2_ref[...], wa2_ref[...]], axis=0)
    dw = _mm3(jnp.where(head0, jnp.tanh(la), 0.0), w2)
    da = _mm3(jnp.where(head0, 0.0, la), w2)
    logw = -EXP_M05 * _sigmoid(wd0_ref[...] + dw)
    a = _sigmoid(a0_ref[...] + da)

    er = lax.broadcasted_iota(jnp.int32, (LANES, LANES), 0)
    ec = lax.broadcasted_iota(jnp.int32, (LANES, LANES), 1)
    e2 = jnp.where((er < RWKV_HEAD) == (ec < RWKV_HEAD), 1.0, 0.0).astype(BF16)

    def head_sums(x):
        return jnp.concatenate(
            [_sum_heads(x[:, p * LANES:(p + 1) * LANES], e2) for p in range(N_PAIR)], axis=1)

    kk = kr * kk_ref[...]
    kk = kk / jnp.maximum(jnp.sqrt(head_sums(kk * kk)), 1e-12)
    kw = kr * (1.0 + (a - 1.0) * ka_ref[...])
    b = kk * a

    tr = lax.broadcasted_iota(jnp.int32, (L, L), 0)
    tc = lax.broadcasted_iota(jnp.int32, (L, L), 1)
    tril = jnp.where(tr >= tc, 1.0, 0.0).astype(BF16)
    cum = _mm_exact_lhs(tril, logw)
    cum_l = cum[L - 1:L, :]
    gi = jnp.exp(-cum)
    gl = jnp.exp(cum_l - cum)
    strict = er > ec
    incl = er >= ec
    eye = jnp.where(er == ec, 1.0, 0.0)

    def blockdiag(x):
        tiles = []
        for p in range(N_PAIR):
            xp = x[:, p * LANES:(p + 1) * LANES]
            tiles.append(jnp.concatenate([jnp.where(head0, xp, 0.0), jnp.where(head0, 0.0, xp)], axis=0))
        return jnp.stack(tiles).astype(BF16)

    kt = blockdiag(kk * jnp.exp(cum - logw))
    rt = blockdiag(r * jnp.exp(cum))
    ki = blockdiag(kw * gi)
    bi = blockdiag(b * gi)
    kh = blockdiag(kw * gl)
    bh = blockdiag(b * gl)
    vb = blockdiag(v)
    lhs = jnp.concatenate([kt, rt], axis=1)
    ak = _dot(lhs, ki, _BNT)
    ab = _dot(lhs, bi, _BNT)
    a_kk = jnp.where(strict, ak[:, :2 * L], 0.0)
    a_rk = jnp.where(incl, ak[:, 2 * L:], 0.0)
    a_kb = jnp.where(strict, ab[:, :2 * L], 0.0)
    a_rb = jnp.where(incl, ab[:, 2 * L:], 0.0)
    t = eye - a_kb
    qb = (-a_kb).astype(BF16)
    for _ in range(5):
        qb = _dot(qb, qb, _BNN).astype(BF16)
        t = t + _dot(t.astype(BF16), qb, _BNN)
    s0 = bd_ref[...]
    zy = _dot(lhs, s0.astype(BF16), _BNT)
    z = zy[:, :2 * L] + _dot(a_kk.astype(BF16), vb, _BNN)
    u = _dot(t.astype(BF16), z.astype(BF16), _BNN)
    vu = jnp.concatenate([vb, u.astype(BF16)], axis=1)
    yb = zy[:, 2 * L:] + _dot(jnp.concatenate([a_rk, -a_rb], axis=2).astype(BF16), vu, _BNN)
    g_l = jnp.exp(cum_l)
    decay = jnp.stack([g_l[:, p * LANES:(p + 1) * LANES] for p in range(N_PAIR)])
    bd_ref[...] = s0 * decay + _dot(vu, jnp.concatenate([kh, -bh], axis=1), _BTN)
    y = jnp.concatenate([yb[p, :L] + yb[p, L:] for p in range(N_PAIR)], axis=1)

    d = y - head_sums(y) * (1.0 / RWKV_HEAD)
    var = head_sums(d * d) * (1.0 / RWKV_HEAD)
    yn = d * lax.rsqrt(var + RWKV_LN_EPS) * lnw_ref[...] + lnb_ref[...]
    bonus = head_sums(r * kw * rk_ref[...]) * v
    g = gr_ref[...]
    y_ref[...] = ((yn + bonus) * (g * _sigmoid(g))).astype(y_ref.dtype)

    @pl.when(c == nc - 1)
    def _():
        for p in range(N_PAIR):
            st_ref[0, 2 * p] = bd_ref[p, 0:RWKV_HEAD, 0:RWKV_HEAD]
            st_ref[0, 2 * p + 1] = bd_ref[p, RWKV_HEAD:LANES, RWKV_HEAD:LANES]


def _rwkv_mix(sh, rest, s0, prev0, mu, wd0, wd2, a0, wa2, k_k, k_a, r_k, lnw, lnb):
    bsz = s0.shape[0]
    t = sh.shape[0] // bsz
    assert t % L_CHUNK == 0
    nc = t // L_CHUNK
    L = L_CHUNK
    row = lambda b, c: (b * nc + c, 0)
    vec = lambda n: pl.BlockSpec((1, n), lambda b, c: (0, 0))
    full = lambda shape: pltpu.VMEM(shape, F32)
    return pl.pallas_call(
        _rwkv_kernel,
        grid=(bsz, nc),
        in_specs=[
            pl.BlockSpec((L, D_SHIFT), row),
            pl.BlockSpec((L, C_RWKV), row),
            pl.BlockSpec((1, H_RWKV, RWKV_HEAD, RWKV_HEAD), lambda b, c: (b, 0, 0, 0)),
            pl.BlockSpec((1, 1, D_SHIFT), lambda b, c: (b, 0, 0)),
            vec(D_SHIFT), vec(C_RWKV),
            pl.BlockSpec((D_LORA, C_RWKV), lambda b, c: (0, 0)),
            vec(C_RWKV),
            pl.BlockSpec((D_LORA, C_RWKV), lambda b, c: (0, 0)),
            vec(C_RWKV), vec(C_RWKV), vec(C_RWKV), vec(C_RWKV), vec(C_RWKV),
        ],
        out_specs=[
            pl.BlockSpec((L, C_RWKV), row),
            pl.BlockSpec((1, H_RWKV, RWKV_HEAD, RWKV_HEAD), lambda b, c: (b, 0, 0, 0)),
        ],
        out_shape=[
            jax.ShapeDtypeStruct((bsz * t, C_RWKV), BF16),
            jax.ShapeDtypeStruct((bsz, H_RWKV, RWKV_HEAD, RWKV_HEAD), F32),
        ],
        scratch_shapes=[
            full((N_PAIR, LANES, LANES)), full((1, D_SHIFT)),
        ],
        compiler_params=_cparams(("parallel", "arbitrary")),
        name="rwkv_mix",
    )(sh, rest, s0, prev0, mu, wd0, wd2, a0, wa2, k_k, k_a, r_k, lnw, lnb)


def _stack_maps(q):
    lane = lax.broadcasted_iota(jnp.int32, q.shape, 1)
    m0 = lane < DK_DIFF
    return jnp.concatenate([jnp.where(m0, q, 0.0), jnp.where(m0, 0.0, q)], axis=0).astype(BF16)


def _fold_lanes(x, op):
    out = x[:, 0:LANES]
    for c in range(1, x.shape[1] // LANES):
        out = op(out, x[:, c * LANES:(c + 1) * LANES])
    return out


def _softmax_tile(s_ref, p_ref, mx_ref, alpha_ref, m_ref, l_ref, bias_rows):
    nrows, tk = s_ref.shape
    ngroups = nrows // ROW_GROUP

    def group(r):
        r0 = pl.multiple_of(r * ROW_GROUP, ROW_GROUP)
        return r0, pl.ds(r0, ROW_GROUP)

    def lane_max(r, carry):
        r0, rows = group(r)
        mx_ref[rows, :] = _fold_lanes(s_ref[rows, :] + bias_rows(r0), jnp.maximum)
        return carry

    lax.fori_loop(0, ngroups, lane_max, 0, unroll=4)
    m_prev = m_ref[...]
    m_new = jnp.maximum(m_prev, jnp.max(mx_ref[...], axis=-1, keepdims=True))
    alpha_ref[...] = jnp.exp2(m_prev - m_new)
    m_ref[...] = m_new

    def probs(r, carry):
        r0, rows = group(r)
        s = s_ref[rows, :] + bias_rows(r0)
        p = jnp.exp2(s - jnp.concatenate([m_ref[rows, :]] * (tk // LANES), axis=1))
        l_ref[rows, :] = alpha_ref[rows, :] * l_ref[rows, :] + _fold_lanes(p, jnp.add)
        p_ref[rows, :] = p.astype(BF16)
        return carry

    lax.fori_loop(0, ngroups, probs, 0, unroll=2)


def _softmax_reset(m_ref, l_ref, acc_ref):
    m_ref[...] = jnp.full(m_ref.shape, NEG_INF, F32)
    l_ref[...] = jnp.zeros(l_ref.shape, F32)
    acc_ref[...] = jnp.zeros(acc_ref.shape, F32)


def _diff_finish(acc, l, lam, lam_init, subln_g, gate):
    t = acc.shape[0] // 2
    o = acc[:t] / l[:t] - lam * (acc[t:] / l[t:])
    o = o * lax.rsqrt(jnp.mean(o * o, axis=-1, keepdims=True) + SUBLN_EPS) * subln_g * (1.0 - lam_init)
    return o * (gate * _sigmoid(gate))


def _lambda(lq1_ref, lk1_ref, lq2_ref, lk2_ref, lam_init):
    s1 = jnp.sum(lq1_ref[...] * lk1_ref[...], axis=-1, keepdims=True)
    s2 = jnp.sum(lq2_ref[...] * lk2_ref[...], axis=-1, keepdims=True)
    return jnp.exp(s1) - jnp.exp(s2) + lam_init


def _local_bias(slope, nrows, nk, tq):
    i = lax.broadcasted_iota(jnp.int32, (nrows, nk), 0)
    i = jnp.where(i >= tq, i - tq, i)
    j = lax.broadcasted_iota(jnp.int32, (nrows, nk), 1)
    bias = slope * (i - jnp.abs(i - j)).astype(F32)
    shift = CHUNK.bit_length() - 1
    visible = jnp.right_shift(j, shift) <= jnp.right_shift(i, shift)
    return jnp.where(visible, bias, NEG_INF)


def _head_slope(h):
    return lax.bitcast_convert_type(jnp.full((1, 1), (126 - h) << 23, jnp.int32), F32) * LOG2E


def _attn_prompt_kernel(q_ref, k_ref, v_ref, gd_ref, lq1_ref, lk1_ref, lq2_ref, lk2_ref, sg_ref,
                        o_ref, q2_ref, s_ref, p_ref, bias_ref, mx_ref, alpha_ref, m_ref, l_ref, acc_ref,
                        *, tq, tk, lam_init):
    h = pl.program_id(0)
    qi = pl.program_id(1)
    kj = pl.program_id(2)
    slope = _head_slope(h)

    @pl.when((qi == 0) & (kj == 0))
    def _():
        bias_ref[...] = _local_bias(slope, tq, tk, tq)

    @pl.when(kj == 0)
    def _():
        q2_ref[...] = _stack_maps(q_ref[...] * (DK_DIFF ** -0.5 * LOG2E))
        _softmax_reset(m_ref, l_ref, acc_ref)

    def update(bias_rows):
        s_ref[...] = _dot(q2_ref[...], k_ref[...].astype(BF16), _NT)
        _softmax_tile(s_ref, p_ref, mx_ref, alpha_ref, m_ref, l_ref, bias_rows)
        acc_ref[...] = alpha_ref[...] * acc_ref[...] + _dot(p_ref[...], v_ref[...].astype(BF16))

    @pl.when(kj < qi)
    def _():
        j = lax.broadcasted_iota(jnp.int32, (1, tk), 1)
        bias = slope * (j - (qi - kj) * tq).astype(F32)
        update(lambda r0: bias)

    @pl.when(kj == qi)
    def _():
        update(lambda r0: bias_ref[pl.ds(pl.multiple_of(lax.rem(r0, tq), ROW_GROUP), ROW_GROUP), :])
        lam = _lambda(lq1_ref, lk1_ref, lq2_ref, lk2_ref, lam_init)
        l = jnp.sum(l_ref[...], axis=-1, keepdims=True)
        o_ref[...] = _diff_finish(acc_ref[...], l, lam, lam_init, sg_ref[...], gd_ref[...]).astype(o_ref.dtype)


def _attn_prompt(rest, lq1, lk1, lq2, lk2, subln_g, lam_init, tq):
    t = rest.shape[0]
    assert t % tq == 0 and tq % CHUNK == 0
    nq = t // tq
    qcol, kcol, vcol, gcol = (C_RWKV // DV_DIFF + n * H_DIFF for n in range(4))
    vec = lambda n: pl.BlockSpec((1, n), lambda h, i, j: (0, 0))
    kern = functools.partial(_attn_prompt_kernel, tq=tq, tk=tq, lam_init=lam_init)
    return pl.pallas_call(
        kern,
        grid=(H_DIFF, nq, nq),
        in_specs=[
            pl.BlockSpec((tq, DV_DIFF), lambda h, i, j: (i, qcol + h)),
            pl.BlockSpec((tq, DV_DIFF), lambda h, i, j: (jnp.minimum(j, i), kcol + h)),
            pl.BlockSpec((tq, DV_DIFF), lambda h, i, j: (jnp.minimum(j, i), vcol + h)),
            pl.BlockSpec((tq, DV_DIFF), lambda h, i, j: (i, gcol + h)),
            vec(DK_DIFF), vec(DK_DIFF), vec(DK_DIFF), vec(DK_DIFF), vec(DV_DIFF),
        ],
        out_specs=pl.BlockSpec((tq, DV_DIFF), lambda h, i, j: (i, h)),
        out_shape=jax.ShapeDtypeStruct((t, C_DIFF), BF16),
        scratch_shapes=[
            pltpu.VMEM((2 * tq, DV_DIFF), BF16),
            pltpu.VMEM((2 * tq, tq), F32),
            pltpu.VMEM((2 * tq, tq), BF16),
            pltpu.VMEM((tq, tq), F32),
            pltpu.VMEM((2 * tq, LANES), F32), pltpu.VMEM((2 * tq, LANES), F32),
            pltpu.VMEM((2 * tq, LANES), F32), pltpu.VMEM((2 * tq, LANES), F32),
            pltpu.VMEM((2 * tq, DV_DIFF), F32),
        ],
        compiler_params=_cparams(("arbitrary", "arbitrary", "arbitrary")),
        name="attn_prompt",
    )(rest, rest, rest, rest, lq1, lk1, lq2, lk2, subln_g)


def _attn_sample_kernel(q_ref, kn_ref, vn_ref, gd_ref, ck_ref, cv_ref, lq1_ref, lk1_ref, lq2_ref, lk2_ref, sg_ref,
                        o_ref, q2_ref, s_ref, p_ref, mx_ref, alpha_ref, m_ref, l_ref, acc_ref,
                        *, t, tk, past_len, lam_init):
    kj = pl.program_id(1)
    h = pl.program_id(2)
    nk = pl.num_programs(1) - 1
    slope = _head_slope(h)
    cols = pl.ds(pl.multiple_of(h * DV_DIFF, DV_DIFF), DV_DIFF)
    q2h, mh, lh, acch = q2_ref.at[h], m_ref.at[h], l_ref.at[h], acc_ref.at[h]

    @pl.when(kj == 0)
    def _():
        q2h[...] = _stack_maps(q_ref[:, cols] * (DK_DIFF ** -0.5 * LOG2E))
        _softmax_reset(mh, lh, acch)

    @pl.when(kj < nk)
    def _():
        j = lax.broadcasted_iota(jnp.int32, (1, tk), 1)
        bias = slope * (j + kj * tk - past_len).astype(F32)
        s_ref[...] = _dot(q2h[...], ck_ref[:, h, :].astype(BF16), _NT)
        _softmax_tile(s_ref, p_ref, mx_ref, alpha_ref, mh, lh, lambda r0: bias)
        acch[...] = alpha_ref[...] * acch[...] + _dot(p_ref[...], cv_ref[:, h, :].astype(BF16))

    @pl.when(kj == nk)
    def _():
        s = _dot(q2h[...], kn_ref[:, cols].astype(BF16), _NT) + _local_bias(slope, 2 * t, t, t)
        m_prev = mh[...]
        m_new = jnp.maximum(m_prev, jnp.max(s, axis=-1, keepdims=True))
        alpha = jnp.exp2(m_prev - m_new)
        p = jnp.exp2(s - m_new[:, 0:t])
        l = jnp.sum(alpha * lh[...], axis=-1, keepdims=True) + jnp.sum(p, axis=-1, keepdims=True)
        acc = alpha * acch[...] + _dot(p.astype(BF16), vn_ref[:, cols].astype(BF16))
        lam = _lambda(lq1_ref, lk1_ref, lq2_ref, lk2_ref, lam_init)
        o_ref[:, cols] = _diff_finish(acc, l, lam, lam_init, sg_ref[...], gd_ref[:, cols]).astype(o_ref.dtype)


def _attn_sample(rest, cache_k, cache_v, layer, lq1, lk1, lq2, lk2, subln_g, lam_init, tk):
    bsz, past_len = cache_k.shape[1], cache_k.shape[2]
    t = rest.shape[0] // bsz
    assert t <= CHUNK and past_len % CHUNK == 0 and past_len % tk == 0
    nk = past_len // tk
    vec = lambda n: pl.BlockSpec((1, n), lambda b, j, h: (0, 0))
    new = lambda col: pl.BlockSpec((t, C_DIFF), lambda b, j, h: (b, col))
    past = pl.BlockSpec((None, None, tk, H_DIFF, DV_DIFF),
                        lambda b, j, h: (layer, b, jnp.minimum(j, nk - 1), 0, 0))
    kern = functools.partial(_attn_sample_kernel, t=t, tk=tk, past_len=past_len, lam_init=lam_init)
    return pl.pallas_call(
        kern,
        grid=(bsz, nk + 1, H_DIFF),
        in_specs=[new(1), new(2), new(3), new(4), past, past,
                  vec(DK_DIFF), vec(DK_DIFF), vec(DK_DIFF), vec(DK_DIFF), vec(DV_DIFF)],
        out_specs=pl.BlockSpec((t, C_DIFF), lambda b, j, h: (b, 0)),
        out_shape=jax.ShapeDtypeStruct((bsz * t, C_DIFF), BF16),
        scratch_shapes=[
            pltpu.VMEM((H_DIFF, 2 * t, DV_DIFF), BF16),
            pltpu.VMEM((2 * t, tk), F32),
            pltpu.VMEM((2 * t, tk), BF16),
            pltpu.VMEM((2 * t, LANES), F32), pltpu.VMEM((2 * t, LANES), F32),
            pltpu.VMEM((H_DIFF, 2 * t, LANES), F32), pltpu.VMEM((H_DIFF, 2 * t, LANES), F32),
            pltpu.VMEM((H_DIFF, 2 * t, DV_DIFF), F32),
        ],
        compiler_params=_cparams(("parallel", "arbitrary", "arbitrary")),
        name="attn_sample",
    )(rest, rest, rest, rest, cache_k, cache_v, lq1, lk1, lq2, lk2, subln_g)


def kernel(x_prompt, x_sample, cache_k, cache_v, state_wkv, state_shift, norm_g, w_in, shift_mu, w_decay0, w_decay2,
           a0, w_a2, k_k, k_a, r_k, lnx_w, lnx_b, lam_q1, lam_k1, lam_q2, lam_k2, subln_g, w_out, final_g):
    depth = w_in.shape[0]
    bp, tp, d = x_prompt.shape
    bs, ts, _ = x_sample.shape
    assert bp == 1, "the prompt attention kernel handles one sequence"

    xp = x_prompt.reshape(bp * tp, d)
    xs = x_sample.reshape(bs * ts, d)
    w_in_b = w_in.astype(BF16)
    w_out_b = w_out.astype(BF16)
    zero_state = jnp.zeros((bp, H_RWKV, RWKV_HEAD, RWKV_HEAD), state_wkv.dtype)
    zero_shift = jnp.zeros((bp, 1, D_SHIFT), state_shift.dtype)
    row = lambda a: a.reshape(1, -1)

    outs = {name: [] for name in ("kp", "vp", "sp", "hp", "ks", "vs", "ss", "hs")}
    for l in range(depth):
        lam_init = _lambda_init(l)
        g = row(norm_g[l])
        w_sh = w_in_b[l, :, :D_SHIFT]
        w_rest = w_in_b[l, :, D_SHIFT:]
        rw = (row(shift_mu[l]), row(w_decay0[l]), w_decay2[l], row(a0[l]), w_a2[l], row(k_k[l]), row(k_a[l]),
              row(r_k[l]), row(lnx_w[l]), row(lnx_b[l]))
        lam = (row(lam_q1[l]), row(lam_k1[l]), row(lam_q2[l]), row(lam_k2[l]), row(subln_g[l]))

        def stream(x, bsz, s0, prev0, attn):
            sh = _norm_matmul(x, g, w_sh, 512, 640)
            rest = _norm_matmul(x, g, w_rest, 512, 1024)
            y_r, s_t = _rwkv_mix(sh, rest, s0, prev0, *rw)
            y_d = attn(rest)
            x = _out_matmul(x, y_r, y_d, w_out_b[l], 512, 1024)
            t = sh.shape[0] // bsz
            k_new = rest[:, C_RWKV + C_DIFF:C_RWKV + 2 * C_DIFF].reshape(bsz, t, H_DIFF, 2 * DK_DIFF)
            v_new = rest[:, C_RWKV + 2 * C_DIFF:C_RWKV + 3 * C_DIFF].reshape(bsz, t, H_DIFF, DV_DIFF)
            shift = sh.reshape(bsz, t, D_SHIFT)[:, -1]
            return x, k_new, v_new, s_t, shift

        xp, kn, vn, sn, hn = stream(
            xp, bp, zero_state, zero_shift,
            lambda rest: _attn_prompt(rest, *lam, lam_init, 512))
        outs["kp"].append(kn); outs["vp"].append(vn); outs["sp"].append(sn); outs["hp"].append(hn)
        xs, kn, vn, sn, hn = stream(
            xs, bs, state_wkv[l], state_shift[l].reshape(bs, 1, D_SHIFT),
            lambda rest: _attn_sample(rest, cache_k, cache_v, l, *lam, lam_init, 1024))
        outs["ks"].append(kn); outs["vs"].append(vn); outs["ss"].append(sn); outs["hs"].append(hn)

    yp = _final_norm(xp, row(final_g), 512).reshape(bp, tp, d)
    ys = _final_norm(xs, row(final_g), 512).reshape(bs, ts, d)
    st = lambda name: jnp.stack(outs[name])
    return (yp, ys, st("kp"), st("vp"), st("sp"), st("hp"), st("ks"), st("vs"), st("ss"), st("hs"))
```

```python
import functools
import math

import jax
import jax.numpy as jnp
from jax import lax
from jax.experimental import pallas as pl
from jax.experimental.pallas import tpu as pltpu

F32 = jnp.float32
BF16 = jnp.bfloat16

D_MODEL = 2048
CHUNK = 64
C_RWKV = 1024
RWKV_HEAD = 64
H_RWKV = C_RWKV // RWKV_HEAD
N_PAIR = H_RWKV // 2
D_LORA = 64
D_SHIFT = 3 * C_RWKV + 2 * D_LORA
C_DIFF = 1024
H_DIFF = 8
DV_DIFF = 128
DK_DIFF = 64
D_REST = C_RWKV + 4 * C_DIFF
NORM_EPS = 1e-6
RWKV_LN_EPS = 64e-5
SUBLN_EPS = 1e-5
NEG_INF = -1e30
LANES = 128
L_CHUNK = 64
EXP_M05 = math.exp(-0.5)
LOG2E = math.log2(math.e)
ROW_GROUP = 16
VMEM_LIMIT = 48 * 1024 * 1024


def _lambda_init(l):
    return 0.8 - 0.6 * math.exp(-0.3 * l)


def _cparams(sem):
    return pltpu.CompilerParams(dimension_semantics=sem, vmem_limit_bytes=VMEM_LIMIT)


_NN = (((1,), (0,)), ((), ()))
_NT = (((1,), (1,)), ((), ()))
_TN = (((0,), (0,)), ((), ()))
_BNN = (((2,), (1,)), ((0,), (0,)))
_BNT = (((2,), (2,)), ((0,), (0,)))
_BTN = (((1,), (1,)), ((0,), (0,)))


def _dot(a, b, dims=_NN):
    return lax.dot_general(a, b, dims, preferred_element_type=F32)


def _split2(x):
    hi = x.astype(BF16)
    lo = (x - hi.astype(F32)).astype(BF16)
    return hi, lo


def _split3(x):
    hi = x.astype(BF16)
    r1 = x - hi.astype(F32)
    mid = r1.astype(BF16)
    lo = (r1 - mid.astype(F32)).astype(BF16)
    return hi, mid, lo


def _mm3(a, b, dims=_NN):
    ah, al = _split2(a)
    bh, bl = _split2(b)
    return _dot(ah, bh, dims) + (_dot(ah, bl, dims) + _dot(al, bh, dims))


def _mm_exact_lhs(a_bf16, b, dims=_NN):
    bh, bm, bl = _split3(b)
    return _dot(a_bf16, bh, dims) + (_dot(a_bf16, bm, dims) + _dot(a_bf16, bl, dims))


def _sum_heads(a, e2):
    return _dot(a.astype(BF16), e2)


def _sigmoid(x):
    return 1.0 / (1.0 + jnp.exp(-x))


def _norm_matmul_kernel(x_ref, g_ref, w_ref, o_ref, h_ref):
    @pl.when(pl.program_id(1) == 0)
    def _():
        x = x_ref[...]
        ms = jnp.mean(x * x, axis=-1, keepdims=True)
        h_ref[...] = (x * lax.rsqrt(ms + NORM_EPS) * g_ref[...]).astype(BF16)

    o_ref[...] = jnp.dot(h_ref[...], w_ref[...], preferred_element_type=F32)


def _norm_matmul(x, g, w, tm, tn):
    m, d = x.shape
    n = w.shape[1]
    return pl.pallas_call(
        _norm_matmul_kernel,
        grid=(m // tm, n // tn),
        in_specs=[
            pl.BlockSpec((tm, d), lambda i, j: (i, 0)),
            pl.BlockSpec((1, d), lambda i, j: (0, 0)),
            pl.BlockSpec((d, tn), lambda i, j: (0, j)),
        ],
        out_specs=pl.BlockSpec((tm, tn), lambda i, j: (i, j)),
        out_shape=jax.ShapeDtypeStruct((m, n), F32),
        scratch_shapes=[pltpu.VMEM((tm, d), BF16)],
        compiler_params=_cparams(("parallel", "arbitrary")),
        name="norm_matmul",
    )(x, g, w)


def _final_norm_kernel(x_ref, g_ref, o_ref):
    x = x_ref[...]
    ms = jnp.mean(x * x, axis=-1, keepdims=True)
    o_ref[...] = x * lax.rsqrt(ms + NORM_EPS) * g_ref[...]


def _final_norm(x, g, tm):
    m, d = x.shape
    return pl.pallas_call(
        _final_norm_kernel,
        grid=(m // tm,),
        in_specs=[pl.BlockSpec((tm, d), lambda i: (i, 0)), pl.BlockSpec((1, d), lambda i: (0, 0))],
        out_specs=pl.BlockSpec((tm, d), lambda i: (i, 0)),
        out_shape=jax.ShapeDtypeStruct((m, d), F32),
        compiler_params=_cparams(("parallel",)),
        name="final_norm",
    )(x, g)


def _out_matmul_kernel(x_ref, yr_ref, yd_ref, wr_ref, wd_ref, o_ref):
    acc = jnp.dot(yr_ref[...], wr_ref[...], preferred_element_type=F32)
    acc = acc + jnp.dot(yd_ref[...], wd_ref[...], preferred_element_type=F32)
    o_ref[...] = x_ref[...] + acc


def _out_matmul(x, y_r, y_d, w_out, tm, tn):
    m, d = x.shape
    return pl.pallas_call(
        _out_matmul_kernel,
        grid=(m // tm, d // tn),
        in_specs=[
            pl.BlockSpec((tm, tn), lambda i, j: (i, j)),
            pl.BlockSpec((tm, C_RWKV), lambda i, j: (i, 0)),
            pl.BlockSpec((tm, C_DIFF), lambda i, j: (i, 0)),
            pl.BlockSpec((C_RWKV, tn), lambda i, j: (0, j)),
            pl.BlockSpec((C_DIFF, tn), lambda i, j: (1, j)),
        ],
        out_specs=pl.BlockSpec((tm, tn), lambda i, j: (i, j)),
        out_shape=jax.ShapeDtypeStruct((m, d), F32),
        compiler_params=_cparams(("parallel", "parallel")),
        name="out_matmul",
    )(x, y_r, y_d, w_out, w_out)


def _rwkv_kernel(sh_ref, gr_ref, s0_ref, p0_ref, mu_ref, wd0_ref, wd2_ref, a0_ref, wa2_ref,
                 kk_ref, ka_ref, rk_ref, lnw_ref, lnb_ref,
                 y_ref, st_ref,
                 bd_ref, prev_ref):
    c = pl.program_id(1)
    nc = pl.num_programs(1)
    L = L_CHUNK
    C = C_RWKV

    @pl.when(c == 0)
    def _():
        prev_ref[...] = p0_ref[0]
        for p in range(N_PAIR):
            bd_ref[p] = jnp.zeros((LANES, LANES), F32)
            bd_ref[p, 0:RWKV_HEAD, 0:RWKV_HEAD] = s0_ref[0, 2 * p]
            bd_ref[p, RWKV_HEAD:LANES, RWKV_HEAD:LANES] = s0_ref[0, 2 * p + 1]

    sh = sh_ref[...]
    row0 = lax.broadcasted_iota(jnp.int32, sh.shape, 0) == 0
    prev = jnp.where(row0, prev_ref[...], pltpu.roll(sh, 1, 0))
    xs = sh + (prev - sh) * mu_ref[...]
    prev_ref[...] = sh_ref[L - 1:L, :]

    r = xs[:, 0:C]
    kr = xs[:, C:2 * C]
    v = xs[:, 2 * C:3 * C]
    la = xs[:, 3 * C:3 * C + LANES]

    lane = lax.broadcasted_iota(jnp.int32, (L, LANES), 1)
    head0 = lane < RWKV_HEAD
    w2 = jnp.concatenate([wd2_ref[...], wa2_ref[...]], axis=0)
    dw = _mm3(jnp.where(head0, jnp.tanh(la), 0.0), w2)
    da = _mm3(jnp.where(head0, 0.0, la), w2)
    logw = -EXP_M05 * _sigmoid(wd0_ref[...] + dw)
    a = _sigmoid(a0_ref[...] + da)

    er = lax.broadcasted_iota(jnp.int32, (LANES, LANES), 0)
    ec = lax.broadcasted_iota(jnp.int32, (LANES, LANES), 1)
    e2 = jnp.where((er < RWKV_HEAD) == (ec < RWKV_HEAD), 1.0, 0.0).astype(BF16)

    def head_sums(x):
        return jnp.concatenate(
            [_sum_heads(x[:, p * LANES:(p + 1) * LANES], e2) for p in range(N_PAIR)], axis=1)

    kk = kr * kk_ref[...]
    kk = kk / jnp.maximum(jnp.sqrt(head_sums(kk * kk)), 1e-12)
    kw = kr * (1.0 + (a - 1.0) * ka_ref[...])
    b = kk * a

    tr = lax.broadcasted_iota(jnp.int32, (L, L), 0)
    tc = lax.broadcasted_iota(jnp.int32, (L, L), 1)
    tril = jnp.where(tr >= tc, 1.0, 0.0).astype(BF16)
    cum = _mm_exact_lhs(tril, logw)
    cum_l = cum[L - 1:L, :]
    gi = jnp.exp(-cum)
    gl = jnp.exp(cum_l - cum)
    strict = er > ec
    incl = er >= ec
    eye = jnp.where(er == ec, 1.0, 0.0)

    def blockdiag(x):
        tiles = []
        for p in range(N_PAIR):
            xp = x[:, p * LANES:(p + 1) * LANES]
            tiles.append(jnp.concatenate([jnp.where(head0, xp, 0.0), jnp.where(head0, 0.0, xp)], axis=0))
        return jnp.stack(tiles).astype(BF16)

    kt = blockdiag(kk * jnp.exp(cum - logw))
    rt = blockdiag(r * jnp.exp(cum))
    ki = blockdiag(kw * gi)
    bi = blockdiag(b * gi)
    kh = blockdiag(kw * gl)
    bh = blockdiag(b * gl)
    vb = blockdiag(v)
    lhs = jnp.concatenate([kt, rt], axis=1)
    ak = _dot(lhs, ki, _BNT)
    ab = _dot(lhs, bi, _BNT)
    a_kk = jnp.where(strict, ak[:, :2 * L], 0.0)
    a_rk = jnp.where(incl, ak[:, 2 * L:], 0.0)
    a_kb = jnp.where(strict, ab[:, :2 * L], 0.0)
    a_rb = jnp.where(incl, ab[:, 2 * L:], 0.0)
    t = eye - a_kb
    qb = (-a_kb).astype(BF16)
    for _ in range(5):
        qb = _dot(qb, qb, _BNN).astype(BF16)
        t = t + _dot(t.astype(BF16), qb, _BNN)
    s0 = bd_ref[...]
    zy = _dot(lhs, s0.astype(BF16), _BNT)
    z = zy[:, :2 * L] + _dot(a_kk.astype(BF16), vb, _BNN)
    u = _dot(t.astype(BF16), z.astype(BF16), _BNN)
    vu = jnp.concatenate([vb, u.astype(BF16)], axis=1)
    yb = zy[:, 2 * L:] + _dot(jnp.concatenate([a_rk, -a_rb], axis=2).astype(BF16), vu, _BNN)
    g_l = jnp.exp(cum_l)
    decay = jnp.stack([g_l[:, p * LANES:(p + 1) * LANES] for p in range(N_PAIR)])
    bd_ref[...] = s0 * decay + _dot(vu, jnp.concatenate([kh, -bh], axis=1), _BTN)
    y = jnp.concatenate([yb[p, :L] + yb[p, L:] for p in range(N_PAIR)], axis=1)

    d = y - head_sums(y) * (1.0 / RWKV_HEAD)
    var = head_sums(d * d) * (1.0 / RWKV_HEAD)
    yn = d * lax.rsqrt(var + RWKV_LN_EPS) * lnw_ref[...] + lnb_ref[...]
    bonus = head_sums(r * kw * rk_ref[...]) * v
    g = gr_ref[...]
    y_ref[...] = ((yn + bonus) * (g * _sigmoid(g))).astype(y_ref.dtype)

    @pl.when(c == nc - 1)
    def _():
        for p in range(N_PAIR):
            st_ref[0, 2 * p] = bd_ref[p, 0:RWKV_HEAD, 0:RWKV_HEAD]
            st_ref[0, 2 * p + 1] = bd_ref[p, RWKV_HEAD:LANES, RWKV_HEAD:LANES]


def _rwkv_mix(sh, rest, s0, prev0, mu, wd0, wd2, a0, wa2, k_k, k_a, r_k, lnw, lnb):
    bsz = s0.shape[0]
    t = sh.shape[0] // bsz
    assert t % L_CHUNK == 0
    nc = t // L_CHUNK
    L = L_CHUNK
    row = lambda b, c: (b * nc + c, 0)
    vec = lambda n: pl.BlockSpec((1, n), lambda b, c: (0, 0))
    full = lambda shape: pltpu.VMEM(shape, F32)
    return pl.pallas_call(
        _rwkv_kernel,
        grid=(bsz, nc),
        in_specs=[
            pl.BlockSpec((L, D_SHIFT), row),
            pl.BlockSpec((L, C_RWKV), row),
            pl.BlockSpec((1, H_RWKV, RWKV_HEAD, RWKV_HEAD), lambda b, c: (b, 0, 0, 0)),
            pl.BlockSpec((1, 1, D_SHIFT), lambda b, c: (b, 0, 0)),
            vec(D_SHIFT), vec(C_RWKV),
            pl.BlockSpec((D_LORA, C_RWKV), lambda b, c: (0, 0)),
            vec(C_RWKV),
            pl.BlockSpec((D_LORA, C_RWKV), lambda b, c: (0, 0)),
            vec(C_RWKV), vec(C_RWKV), vec(C_RWKV), vec(C_RWKV), vec(C_RWKV),
        ],
        out_specs=[
            pl.BlockSpec((L, C_RWKV), row),
            pl.BlockSpec((1, H_RWKV, RWKV_HEAD, RWKV_HEAD), lambda b, c: (b, 0, 0, 0)),
        ],
        out_shape=[
            jax.ShapeDtypeStruct((bsz * t, C_RWKV), BF16),
            jax.ShapeDtypeStruct((bsz, H_RWKV, RWKV_HEAD, RWKV_HEAD), F32),
        ],
        scratch_shapes=[
            full((N_PAIR, LANES, LANES)), full((1, D_SHIFT)),
        ],
        compiler_params=_cparams(("parallel", "arbitrary")),
        name="rwkv_mix",
    )(sh, rest, s0, prev0, mu, wd0, wd2, a0, wa2, k_k, k_a, r_k, lnw, lnb)


def _stack_maps(q):
    lane = lax.broadcasted_iota(jnp.int32, q.shape, 1)
    m0 = lane < DK_DIFF
    return jnp.concatenate([jnp.where(m0, q, 0.0), jnp.where(m0, 0.0, q)], axis=0).astype(BF16)


def _fold_lanes(x, op):
    out = x[:, 0:LANES]
    for c in range(1, x.shape[1] // LANES):
        out = op(out, x[:, c * LANES:(c + 1) * LANES])
    return out


def _softmax_tile(s_ref, p_ref, mx_ref, alpha_ref, m_ref, l_ref, bias_rows):
    nrows, tk = s_ref.shape
    ngroups = nrows // ROW_GROUP

    def group(r):
        r0 = r * ROW_GROUP
        return r0, pl.ds(r0, ROW_GROUP)

    def lane_max(r, carry):
        r0, rows = group(r)
        mx_ref[rows, :] = _fold_lanes(s_ref[rows, :] + bias_rows(r0), jnp.maximum)
        return carry

    for r in range(ngroups):
        lane_max(r, 0)
    m_prev = m_ref[...]
    m_new = jnp.maximum(m_prev, jnp.max(mx_ref[...], axis=-1, keepdims=True))
    alpha_ref[...] = jnp.exp2(m_prev - m_new)
    m_ref[...] = m_new

    def probs(r, carry):
        r0, rows = group(r)
        s = s_ref[rows, :] + bias_rows(r0)
        p = jnp.exp2(s - jnp.concatenate([m_ref[rows, :]] * (tk // LANES), axis=1))
        l_ref[rows, :] = alpha_ref[rows, :] * l_ref[rows, :] + _fold_lanes(p, jnp.add)
        p_ref[rows, :] = p.astype(BF16)
        return carry

    for r in range(ngroups):
        probs(r, 0)


def _softmax_reset(m_ref, l_ref, acc_ref):
    m_ref[...] = jnp.full(m_ref.shape, NEG_INF, F32)
    l_ref[...] = jnp.zeros(l_ref.shape, F32)
    acc_ref[...] = jnp.zeros(acc_ref.shape, F32)


def _diff_finish(acc, l, lam, lam_init, subln_g, gate):
    t = acc.shape[0] // 2
    o = acc[:t] / l[:t] - lam * (acc[t:] / l[t:])
    o = o * lax.rsqrt(jnp.mean(o * o, axis=-1, keepdims=True) + SUBLN_EPS) * subln_g * (1.0 - lam_init)
    return o * (gate * _sigmoid(gate))


def _lambda(lq1_ref, lk1_ref, lq2_ref, lk2_ref, lam_init):
    s1 = jnp.sum(lq1_ref[...] * lk1_ref[...], axis=-1, keepdims=True)
    s2 = jnp.sum(lq2_ref[...] * lk2_ref[...], axis=-1, keepdims=True)
    return jnp.exp(s1) - jnp.exp(s2) + lam_init


def _local_bias(slope, nrows, nk, tq):
    i = lax.broadcasted_iota(jnp.int32, (nrows, nk), 0)
    i = jnp.where(i >= tq, i - tq, i)
    j = lax.broadcasted_iota(jnp.int32, (nrows, nk), 1)
    bias = slope * (i - jnp.abs(i - j)).astype(F32)
    shift = CHUNK.bit_length() - 1
    visible = jnp.right_shift(j, shift) <= jnp.right_shift(i, shift)
    return jnp.where(visible, bias, NEG_INF)


def _head_slope(h):
    return lax.bitcast_convert_type(jnp.full((1, 1), (126 - h) << 23, jnp.int32), F32) * LOG2E


def _attn_prompt_kernel(qi_ref, kj_ref, q_ref, k_ref, v_ref, gd_ref, lq1_ref, lk1_ref, lq2_ref, lk2_ref, sg_ref,
                        o_ref, q2_ref, s_ref, p_ref, bias_ref, mx_ref, alpha_ref, m_ref, l_ref, acc_ref,
                        *, tq, tk, lam_init):
    h = pl.program_id(0)
    qi = qi_ref[pl.program_id(1)]
    kj = kj_ref[pl.program_id(1)]
    slope = _head_slope(h)

    @pl.when((qi == 0) & (kj == 0))
    def _():
        bias_ref[...] = _local_bias(slope, tq, tk, tq)

    @pl.when(kj == 0)
    def _():
        q2_ref[...] = _stack_maps(q_ref[...] * (DK_DIFF ** -0.5 * LOG2E))
        _softmax_reset(m_ref, l_ref, acc_ref)

    def update(bias_rows):
        s_ref[...] = _dot(q2_ref[...], k_ref[...].astype(BF16), _NT)
        _softmax_tile(s_ref, p_ref, mx_ref, alpha_ref, m_ref, l_ref, bias_rows)
        acc_ref[...] = alpha_ref[...] * acc_ref[...] + _dot(p_ref[...], v_ref[...].astype(BF16))

    @pl.when(kj < qi)
    def _():
        j = lax.broadcasted_iota(jnp.int32, (1, tk), 1)
        bias = slope * (j - (qi - kj) * tq).astype(F32)
        update(lambda r0: bias)

    @pl.when(kj == qi)
    def _():
        update(lambda r0: bias_ref[pl.ds(r0 % tq, ROW_GROUP), :])
        lam = _lambda(lq1_ref, lk1_ref, lq2_ref, lk2_ref, lam_init)
        l = jnp.sum(l_ref[...], axis=-1, keepdims=True)
        o_ref[...] = _diff_finish(acc_ref[...], l, lam, lam_init, sg_ref[...], gd_ref[...]).astype(o_ref.dtype)


def _attn_prompt(rest, lq1, lk1, lq2, lk2, subln_g, lam_init, tq):
    t = rest.shape[0]
    assert t % tq == 0 and tq % CHUNK == 0
    nq = t // tq
    qcol, kcol, vcol, gcol = (C_RWKV // DV_DIFF + n * H_DIFF for n in range(4))
    vec = lambda n: pl.BlockSpec((1, n), lambda h, s, qi, kj: (0, 0))
    kern = functools.partial(_attn_prompt_kernel, tq=tq, tk=tq, lam_init=lam_init)
    qi_tab = jnp.asarray([i for i in range(nq) for _ in range(i + 1)], jnp.int32)
    kj_tab = jnp.asarray([j for i in range(nq) for j in range(i + 1)], jnp.int32)
    grid_spec = pltpu.PrefetchScalarGridSpec(
        num_scalar_prefetch=2,
        grid=(H_DIFF, nq * (nq + 1) // 2),
        in_specs=[
            pl.BlockSpec((tq, DV_DIFF), lambda h, s, qi, kj: (qi[s], qcol + h)),
            pl.BlockSpec((tq, DV_DIFF), lambda h, s, qi, kj: (kj[s], kcol + h)),
            pl.BlockSpec((tq, DV_DIFF), lambda h, s, qi, kj: (kj[s], vcol + h)),
            pl.BlockSpec((tq, DV_DIFF), lambda h, s, qi, kj: (qi[s], gcol + h)),
            vec(DK_DIFF), vec(DK_DIFF), vec(DK_DIFF), vec(DK_DIFF), vec(DV_DIFF),
        ],
        out_specs=pl.BlockSpec((tq, DV_DIFF), lambda h, s, qi, kj: (qi[s], h)),
        scratch_shapes=[
            pltpu.VMEM((2 * tq, DV_DIFF), BF16),
            pltpu.VMEM((2 * tq, tq), F32),
            pltpu.VMEM((2 * tq, tq), BF16),
            pltpu.VMEM((tq, tq), F32),
            pltpu.VMEM((2 * tq, LANES), F32), pltpu.VMEM((2 * tq, LANES), F32),
            pltpu.VMEM((2 * tq, LANES), F32), pltpu.VMEM((2 * tq, LANES), F32),
            pltpu.VMEM((2 * tq, DV_DIFF), F32),
        ],
    )
    return pl.pallas_call(
        kern,
        grid_spec=grid_spec,
        out_shape=jax.ShapeDtypeStruct((t, C_DIFF), BF16),
        compiler_params=_cparams(("arbitrary", "arbitrary")),
        name="attn_prompt",
    )(qi_tab, kj_tab, rest, rest, rest, rest, lq1, lk1, lq2, lk2, subln_g)


def _attn_sample_kernel(q_ref, kn_ref, vn_ref, gd_ref, ck_ref, cv_ref, lq1_ref, lk1_ref, lq2_ref, lk2_ref, sg_ref,
                        o_ref, q2_ref, s_ref, p_ref, mx_ref, alpha_ref, m_ref, l_ref, acc_ref,
                        *, t, tk, past_len, lam_init):
    kj = pl.program_id(1)
    h = pl.program_id(2)
    nk = pl.num_programs(1) - 1
    slope = _head_slope(h)
    cols = pl.ds(pl.multiple_of(h * DV_DIFF, DV_DIFF), DV_DIFF)
    q2h, mh, lh, acch = q2_ref.at[h], m_ref.at[h], l_ref.at[h], acc_ref.at[h]

    @pl.when(kj == 0)
    def _():
        q2h[...] = _stack_maps(q_ref[:, cols] * (DK_DIFF ** -0.5 * LOG2E))
        _softmax_reset(mh, lh, acch)

    @pl.when(kj < nk)
    def _():
        j = lax.broadcasted_iota(jnp.int32, (1, tk), 1)
        bias = slope * (j + kj * tk - past_len).astype(F32)
        head_rows = pl.ds(h, tk, stride=H_DIFF)
        s_ref[...] = _dot(q2h[...], ck_ref[head_rows, :].astype(BF16), _NT)
        _softmax_tile(s_ref, p_ref, mx_ref, alpha_ref, mh, lh, lambda r0: bias)
        acch[...] = alpha_ref[...] * acch[...] + _dot(p_ref[...], cv_ref[head_rows, :].astype(BF16))

    @pl.when(kj == nk)
    def _():
        s = _dot(q2h[...], kn_ref[:, cols].astype(BF16), _NT) + _local_bias(slope, 2 * t, t, t)
        m_prev = mh[...]
        m_new = jnp.maximum(m_prev, jnp.max(s, axis=-1, keepdims=True))
        alpha = jnp.exp2(m_prev - m_new)
        p = jnp.exp2(s - m_new[:, 0:t])
        l = jnp.sum(alpha * lh[...], axis=-1, keepdims=True) + jnp.sum(p, axis=-1, keepdims=True)
        acc = alpha * acch[...] + _dot(p.astype(BF16), vn_ref[:, cols].astype(BF16))
        lam = _lambda(lq1_ref, lk1_ref, lq2_ref, lk2_ref, lam_init)
        o_ref[:, cols] = _diff_finish(acc, l, lam, lam_init, sg_ref[...], gd_ref[:, cols]).astype(o_ref.dtype)


def _attn_sample(rest, cache_k, cache_v, layer, lq1, lk1, lq2, lk2, subln_g, lam_init, tk):
    depth, bsz, past_len = cache_k.shape[:3]
    cache_k = cache_k.reshape(depth, bsz, past_len * H_DIFF, DV_DIFF)
    cache_v = cache_v.reshape(depth, bsz, past_len * H_DIFF, DV_DIFF)
    t = rest.shape[0] // bsz
    assert t <= CHUNK and past_len % CHUNK == 0 and past_len % tk == 0
    nk = past_len // tk
    vec = lambda n: pl.BlockSpec((1, n), lambda b, j, h: (0, 0))
    new = lambda col: pl.BlockSpec((t, C_DIFF), lambda b, j, h: (b, col))
    past = pl.BlockSpec((None, None, tk * H_DIFF, DV_DIFF),
                        lambda b, j, h: (layer, b, jnp.minimum(j, nk - 1), 0))
    kern = functools.partial(_attn_sample_kernel, t=t, tk=tk, past_len=past_len, lam_init=lam_init)
    return pl.pallas_call(
        kern,
        grid=(bsz, nk + 1, H_DIFF),
        in_specs=[new(1), new(2), new(3), new(4), past, past,
                  vec(DK_DIFF), vec(DK_DIFF), vec(DK_DIFF), vec(DK_DIFF), vec(DV_DIFF)],
        out_specs=pl.BlockSpec((t, C_DIFF), lambda b, j, h: (b, 0)),
        out_shape=jax.ShapeDtypeStruct((bsz * t, C_DIFF), BF16),
        scratch_shapes=[
            pltpu.VMEM((H_DIFF, 2 * t, DV_DIFF), BF16),
            pltpu.VMEM((2 * t, tk), F32),
            pltpu.VMEM((2 * t, tk), BF16),
            pltpu.VMEM((2 * t, LANES), F32), pltpu.VMEM((2 * t, LANES), F32),
            pltpu.VMEM((H_DIFF, 2 * t, LANES), F32), pltpu.VMEM((H_DIFF, 2 * t, LANES), F32),
            pltpu.VMEM((H_DIFF, 2 * t, DV_DIFF), F32),
        ],
        compiler_params=_cparams(("parallel", "arbitrary", "arbitrary")),
        name="attn_sample",
    )(rest, rest, rest, rest, cache_k, cache_v, lq1, lk1, lq2, lk2, subln_g)


def kernel(x_prompt, x_sample, cache_k, cache_v, state_wkv, state_shift, norm_g, w_in, shift_mu, w_decay0, w_decay2,
           a0, w_a2, k_k, k_a, r_k, lnx_w, lnx_b, lam_q1, lam_k1, lam_q2, lam_k2, subln_g, w_out, final_g):
    depth = w_in.shape[0]
    bp, tp, d = x_prompt.shape
    bs, ts, _ = x_sample.shape
    assert bp == 1, "the prompt attention kernel handles one sequence"

    xp = x_prompt.reshape(bp * tp, d)
    xs = x_sample.reshape(bs * ts, d)
    w_in_b = w_in.astype(BF16)
    w_out_b = w_out.astype(BF16)
    zero_state = jnp.zeros((bp, H_RWKV, RWKV_HEAD, RWKV_HEAD), state_wkv.dtype)
    zero_shift = jnp.zeros((bp, 1, D_SHIFT), state_shift.dtype)
    row = lambda a: a.reshape(1, -1)

    outs = {name: [] for name in ("kp", "vp", "sp", "hp", "ks", "vs", "ss", "hs")}
    for l in range(depth):
        lam_init = _lambda_init(l)
        g = row(norm_g[l])
        w_sh = w_in_b[l, :, :D_SHIFT]
        w_rest = w_in_b[l, :, D_SHIFT:]
        rw = (row(shift_mu[l]), row(w_decay0[l]), w_decay2[l], row(a0[l]), w_a2[l], row(k_k[l]), row(k_a[l]),
              row(r_k[l]), row(lnx_w[l]), row(lnx_b[l]))
        lam = (row(lam_q1[l]), row(lam_k1[l]), row(lam_q2[l]), row(lam_k2[l]), row(subln_g[l]))

        def stream(x, bsz, s0, prev0, attn):
            sh = _norm_matmul(x, g, w_sh, 512, 640)
            rest = _norm_matmul(x, g, w_rest, 512, 1024)
            y_r, s_t = _rwkv_mix(sh, rest, s0, prev0, *rw)
            y_d = attn(rest)
            x = _out_matmul(x, y_r, y_d, w_out_b[l], 512, 1024)
            t = sh.shape[0] // bsz
            k_new = rest[:, C_RWKV + C_DIFF:C_RWKV + 2 * C_DIFF].reshape(bsz, t, H_DIFF, 2 * DK_DIFF)
            v_new = rest[:, C_RWKV + 2 * C_DIFF:C_RWKV + 3 * C_DIFF].reshape(bsz, t, H_DIFF, DV_DIFF)
            shift = sh.reshape(bsz, t, D_SHIFT)[:, -1]
            return x, k_new, v_new, s_t, shift

        xp, kn, vn, sn, hn = stream(
            xp, bp, zero_state, zero_shift,
            lambda rest: _attn_prompt(rest, *lam, lam_init, 512))
        outs["kp"].append(kn); outs["vp"].append(vn); outs["sp"].append(sn); outs["hp"].append(hn)
        xs, kn, vn, sn, hn = stream(
            xs, bs, state_wkv[l], state_shift[l].reshape(bs, 1, D_SHIFT),
            lambda rest: _attn_sample(rest, cache_k, cache_v, l, *lam, lam_init, 1024))
        outs["ks"].append(kn); outs["vs"].append(vn); outs["ss"].append(sn); outs["hs"].append(hn)

    yp = _final_norm(xp, row(final_g), 512).reshape(bp, tp, d)
    ys = _final_norm(xs, row(final_g), 512).reshape(bs, ts, d)
    st = lambda name: jnp.stack(outs[name])
    return (yp, ys, st("kp"), st("vp"), st("sp"), st("hp"), st("ks"), st("vs"), st("ss"), st("hs"))
```

```python
import functools
import math

import jax
import jax.numpy as jnp
from jax import lax
from jax.experimental import pallas as pl
from jax.experimental.pallas import tpu as pltpu

F32 = jnp.float32
BF16 = jnp.bfloat16

D_MODEL = 2048
CHUNK = 64
C_RWKV = 1024
RWKV_HEAD = 64
H_RWKV = C_RWKV // RWKV_HEAD
N_PAIR = H_RWKV // 2
D_LORA = 64
D_SHIFT = 3 * C_RWKV + 2 * D_LORA
C_DIFF = 1024
H_DIFF = 8
DV_DIFF = 128
DK_DIFF = 64
D_REST = C_RWKV + 4 * C_DIFF
NORM_EPS = 1e-6
RWKV_LN_EPS = 64e-5
SUBLN_EPS = 1e-5
NEG_INF = -1e30
LANES = 128
L_CHUNK = 64
EXP_M05 = math.exp(-0.5)
LOG2E = math.log2(math.e)
ROW_GROUP = 16
VMEM_LIMIT = 48 * 1024 * 1024


def _lambda_init(l):
    return 0.8 - 0.6 * math.exp(-0.3 * l)


def _cparams(sem, vmem_limit=VMEM_LIMIT):
    return pltpu.CompilerParams(dimension_semantics=sem, vmem_limit_bytes=vmem_limit)


_NN = (((1,), (0,)), ((), ()))
_NT = (((1,), (1,)), ((), ()))
_TN = (((0,), (0,)), ((), ()))
_BNN = (((2,), (1,)), ((0,), (0,)))
_BNT = (((2,), (2,)), ((0,), (0,)))
_BTN = (((1,), (1,)), ((0,), (0,)))


def _dot(a, b, dims=_NN):
    return lax.dot_general(a, b, dims, preferred_element_type=F32)


def _split2(x):
    hi = x.astype(BF16)
    lo = (x - hi.astype(F32)).astype(BF16)
    return hi, lo


def _split3(x):
    hi = x.astype(BF16)
    r1 = x - hi.astype(F32)
    mid = r1.astype(BF16)
    lo = (r1 - mid.astype(F32)).astype(BF16)
    return hi, mid, lo


def _mm3(a, b, dims=_NN):
    ah, al = _split2(a)
    bh, bl = _split2(b)
    return _dot(ah, bh, dims) + (_dot(ah, bl, dims) + _dot(al, bh, dims))


def _mm_exact_lhs(a_bf16, b, dims=_NN):
    bh, bm, bl = _split3(b)
    return _dot(a_bf16, bh, dims) + (_dot(a_bf16, bm, dims) + _dot(a_bf16, bl, dims))


def _sum_heads(a, e2):
    return _dot(a.astype(BF16), e2)


def _sigmoid(x):
    return 1.0 / (1.0 + jnp.exp(-x))


def _norm_matmul_kernel(x_ref, g_ref, w_ref, o_ref, h_ref):
    @pl.when(pl.program_id(1) == 0)
    def _():
        x = x_ref[...]
        ms = jnp.mean(x * x, axis=-1, keepdims=True)
        h_ref[...] = (x * lax.rsqrt(ms + NORM_EPS) * g_ref[...]).astype(BF16)

    o_ref[...] = jnp.dot(h_ref[...], w_ref[...], preferred_element_type=F32)


def _norm_matmul(x, g, w, tm, tn):
    m, d = x.shape
    n = w.shape[1]
    return pl.pallas_call(
        _norm_matmul_kernel,
        grid=(m // tm, n // tn),
        in_specs=[
            pl.BlockSpec((tm, d), lambda i, j: (i, 0)),
            pl.BlockSpec((1, d), lambda i, j: (0, 0)),
            pl.BlockSpec((d, tn), lambda i, j: (0, j)),
        ],
        out_specs=pl.BlockSpec((tm, tn), lambda i, j: (i, j)),
        out_shape=jax.ShapeDtypeStruct((m, n), F32),
        scratch_shapes=[pltpu.VMEM((tm, d), BF16)],
        compiler_params=_cparams(("parallel", "arbitrary")),
        name="norm_matmul",
    )(x, g, w)


def _proj_rest_kernel(x_ref, g_ref, w_ref, kin_ref, vin_ref, o_ref, k_ref, v_ref, h_ref):
    del kin_ref, vin_ref
    j = pl.program_id(1)

    @pl.when(j == 0)
    def _():
        x = x_ref[...]
        ms = jnp.mean(x * x, axis=-1, keepdims=True)
        h_ref[...] = (x * lax.rsqrt(ms + NORM_EPS) * g_ref[...]).astype(BF16)

    acc = jnp.dot(h_ref[...], w_ref[...], preferred_element_type=F32)

    @pl.when((j != _KCOL) & (j != _VCOL))
    def _():
        o_ref[...] = acc

    @pl.when(j == _KCOL)
    def _():
        k_ref[...] = acc

    @pl.when(j == _VCOL)
    def _():
        v_ref[...] = acc


_KCOL, _VCOL = 2, 3


def _proj_rest(x, g, w, k_all, v_all, layer, tm):
    m, d = x.shape
    tn = C_DIFF
    assert w.shape[1] == 5 * tn and C_RWKV == tn
    ocol = lambda j: jnp.minimum(j, 1) + j // 4
    return pl.pallas_call(
        _proj_rest_kernel,
        grid=(m // tm, 5),
        in_specs=[
            pl.BlockSpec((tm, d), lambda i, j: (i, 0)),
            pl.BlockSpec((1, d), lambda i, j: (0, 0)),
            pl.BlockSpec((d, tn), lambda i, j: (0, j)),
            pl.BlockSpec(memory_space=pl.ANY),
            pl.BlockSpec(memory_space=pl.ANY),
        ],
        out_specs=[
            pl.BlockSpec((tm, tn), lambda i, j: (i, ocol(j))),
            pl.BlockSpec((None, tm, tn), lambda i, j: (layer, i, 0)),
            pl.BlockSpec((None, tm, tn), lambda i, j: (layer, i, 0)),
        ],
        out_shape=[
            jax.ShapeDtypeStruct((m, 3 * tn), F32),
            jax.ShapeDtypeStruct(k_all.shape, F32),
            jax.ShapeDtypeStruct(v_all.shape, F32),
        ],
        input_output_aliases={3: 1, 4: 2},
        scratch_shapes=[pltpu.VMEM((tm, d), BF16)],
        compiler_params=_cparams(("parallel", "arbitrary"),
                                 8 * tm * d + 4 * d * tn + 24 * tm * tn + 2 * tm * d + (4 << 20)),
        name="proj_rest",
    )(x, g, w, k_all, v_all)


def _final_norm_kernel(x_ref, g_ref, o_ref):
    x = x_ref[...]
    ms = jnp.mean(x * x, axis=-1, keepdims=True)
    o_ref[...] = x * lax.rsqrt(ms + NORM_EPS) * g_ref[...]


def _final_norm(x, g, tm):
    m, d = x.shape
    return pl.pallas_call(
        _final_norm_kernel,
        grid=(m // tm,),
        in_specs=[pl.BlockSpec((tm, d), lambda i: (i, 0)), pl.BlockSpec((1, d), lambda i: (0, 0))],
        out_specs=pl.BlockSpec((tm, d), lambda i: (i, 0)),
        out_shape=jax.ShapeDtypeStruct((m, d), F32),
        compiler_params=_cparams(("parallel",)),
        name="final_norm",
    )(x, g)


def _out_matmul_kernel(x_ref, yr_ref, yd_ref, wr_ref, wd_ref, o_ref):
    acc = jnp.dot(yr_ref[...], wr_ref[...], preferred_element_type=F32)
    acc = acc + jnp.dot(yd_ref[...], wd_ref[...], preferred_element_type=F32)
    o_ref[...] = x_ref[...] + acc


def _out_matmul(x, y_r, y_d, w_out, tm, tn):
    m, d = x.shape
    return pl.pallas_call(
        _out_matmul_kernel,
        grid=(m // tm, d // tn),
        in_specs=[
            pl.BlockSpec((tm, tn), lambda i, j: (i, j)),
            pl.BlockSpec((tm, C_RWKV), lambda i, j: (i, 0)),
            pl.BlockSpec((tm, C_DIFF), lambda i, j: (i, 0)),
            pl.BlockSpec((C_RWKV, tn), lambda i, j: (0, j)),
            pl.BlockSpec((C_DIFF, tn), lambda i, j: (1, j)),
        ],
        out_specs=pl.BlockSpec((tm, tn), lambda i, j: (i, j)),
        out_shape=jax.ShapeDtypeStruct((m, d), F32),
        compiler_params=_cparams(("parallel", "parallel")),
        name="out_matmul",
    )(x, y_r, y_d, w_out, w_out)


def _rwkv_kernel(sh_ref, gr_ref, s0_ref, p0_ref, mu_ref, wd0_ref, wd2_ref, a0_ref, wa2_ref,
                 kk_ref, ka_ref, rk_ref, lnw_ref, lnb_ref,
                 y_ref, st_ref,
                 bd_ref, prev_ref):
    c = pl.program_id(1)
    nc = pl.num_programs(1)
    L = L_CHUNK
    C = C_RWKV

    @pl.when(c == 0)
    def _():
        prev_ref[...] = p0_ref[0]
        for p in range(N_PAIR):
            bd_ref[p] = jnp.zeros((LANES, LANES), F32)
            bd_ref[p, 0:RWKV_HEAD, 0:RWKV_HEAD] = s0_ref[0, 2 * p]
            bd_ref[p, RWKV_HEAD:LANES, RWKV_HEAD:LANES] = s0_ref[0, 2 * p + 1]

    sh = sh_ref[...]
    row0 = lax.broadcasted_iota(jnp.int32, sh.shape, 0) == 0
    prev = jnp.where(row0, prev_ref[...], pltpu.roll(sh, 1, 0))
    xs = sh + (prev - sh) * mu_ref[...]
    prev_ref[...] = sh_ref[L - 1:L, :]

    r = xs[:, 0:C]
    kr = xs[:, C:2 * C]
    v = xs[:, 2 * C:3 * C]
    la = xs[:, 3 * C:3 * C + LANES]

    lane = lax.broadcasted_iota(jnp.int32, (L, LANES), 1)
    head0 = lane < RWKV_HEAD
    w2 = jnp.concatenate([wd2_ref[...], wa2_ref[...]], axis=0)
    dw = _mm3(jnp.where(head0, jnp.tanh(la), 0.0), w2)
    da = _mm3(jnp.where(head0, 0.0, la), w2)
    logw = -EXP_M05 * _sigmoid(wd0_ref[...] + dw)
    a = _sigmoid(a0_ref[...] + da)

    er = lax.broadcasted_iota(jnp.int32, (LANES, LANES), 0)
    ec = lax.broadcasted_iota(jnp.int32, (LANES, LANES), 1)
    e2 = jnp.where((er < RWKV_HEAD) == (ec < RWKV_HEAD), 1.0, 0.0).astype(BF16)

    def head_sums(x):
        return jnp.concatenate(
            [_sum_heads(x[:, p * LANES:(p + 1) * LANES], e2) for p in range(N_PAIR)], axis=1)

    kk = kr * kk_ref[...]
    kk = kk / jnp.maximum(jnp.sqrt(head_sums(kk * kk)), 1e-12)
    kw = kr * (1.0 + (a - 1.0) * ka_ref[...])
    b = kk * a

    tr = lax.broadcasted_iota(jnp.int32, (L, L), 0)
    tc = lax.broadcasted_iota(jnp.int32, (L, L), 1)
    tril = jnp.where(tr >= tc, 1.0, 0.0).astype(BF16)
    cum = _mm_exact_lhs(tril, logw)
    cum_l = cum[L - 1:L, :]
    gi = jnp.exp(-cum)
    gl = jnp.exp(cum_l - cum)
    strict = er > ec
    incl = er >= ec
    eye = jnp.where(er == ec, 1.0, 0.0)

    def blockdiag(x):
        tiles = []
        for p in range(N_PAIR):
            xp = x[:, p * LANES:(p + 1) * LANES]
            tiles.append(jnp.concatenate([jnp.where(head0, xp, 0.0), jnp.where(head0, 0.0, xp)], axis=0))
        return jnp.stack(tiles).astype(BF16)

    kt = blockdiag(kk * jnp.exp(cum - logw))
    rt = blockdiag(r * jnp.exp(cum))
    ki = blockdiag(kw * gi)
    bi = blockdiag(b * gi)
    kh = blockdiag(kw * gl)
    bh = blockdiag(b * gl)
    vb = blockdiag(v)
    lhs = jnp.concatenate([kt, rt], axis=1)
    ak = _dot(lhs, ki, _BNT)
    ab = _dot(lhs, bi, _BNT)
    a_kk = jnp.where(strict, ak[:, :2 * L], 0.0)
    a_rk = jnp.where(incl, ak[:, 2 * L:], 0.0)
    a_kb = jnp.where(strict, ab[:, :2 * L], 0.0)
    a_rb = jnp.where(incl, ab[:, 2 * L:], 0.0)
    t = eye - a_kb
    qb = (-a_kb).astype(BF16)
    for _ in range(5):
        qb = _dot(qb, qb, _BNN).astype(BF16)
        t = t + _dot(t.astype(BF16), qb, _BNN)
    s0 = bd_ref[...]
    zy = _dot(lhs, s0.astype(BF16), _BNT)
    z = zy[:, :2 * L] + _dot(a_kk.astype(BF16), vb, _BNN)
    u = _dot(t.astype(BF16), z.astype(BF16), _BNN)
    vu = jnp.concatenate([vb, u.astype(BF16)], axis=1)
    yb = zy[:, 2 * L:] + _dot(jnp.concatenate([a_rk, -a_rb], axis=2).astype(BF16), vu, _BNN)
    g_l = jnp.exp(cum_l)
    decay = jnp.stack([g_l[:, p * LANES:(p + 1) * LANES] for p in range(N_PAIR)])
    bd_ref[...] = s0 * decay + _dot(vu, jnp.concatenate([kh, -bh], axis=1), _BTN)
    y = jnp.concatenate([yb[p, :L] + yb[p, L:] for p in range(N_PAIR)], axis=1)

    d = y - head_sums(y) * (1.0 / RWKV_HEAD)
    var = head_sums(d * d) * (1.0 / RWKV_HEAD)
    yn = d * lax.rsqrt(var + RWKV_LN_EPS) * lnw_ref[...] + lnb_ref[...]
    bonus = head_sums(r * kw * rk_ref[...]) * v
    g = gr_ref[...]
    y_ref[...] = ((yn + bonus) * (g * _sigmoid(g))).astype(y_ref.dtype)

    @pl.when(c == nc - 1)
    def _():
        for p in range(N_PAIR):
            st_ref[0, 2 * p] = bd_ref[p, 0:RWKV_HEAD, 0:RWKV_HEAD]
            st_ref[0, 2 * p + 1] = bd_ref[p, RWKV_HEAD:LANES, RWKV_HEAD:LANES]


def _rwkv_mix(sh, rest, s0, prev0, mu, wd0, wd2, a0, wa2, k_k, k_a, r_k, lnw, lnb):
    bsz = s0.shape[0]
    t = sh.shape[0] // bsz
    assert t % L_CHUNK == 0
    nc = t // L_CHUNK
    L = L_CHUNK
    row = lambda b, c: (b * nc + c, 0)
    vec = lambda n: pl.BlockSpec((1, n), lambda b, c: (0, 0))
    full = lambda shape: pltpu.VMEM(shape, F32)
    return pl.pallas_call(
        _rwkv_kernel,
        grid=(bsz, nc),
        in_specs=[
            pl.BlockSpec((L, D_SHIFT), row),
            pl.BlockSpec((L, C_RWKV), row),
            pl.BlockSpec((1, H_RWKV, RWKV_HEAD, RWKV_HEAD), lambda b, c: (b, 0, 0, 0)),
            pl.BlockSpec((1, 1, D_SHIFT), lambda b, c: (b, 0, 0)),
            vec(D_SHIFT), vec(C_RWKV),
            pl.BlockSpec((D_LORA, C_RWKV), lambda b, c: (0, 0)),
            vec(C_RWKV),
            pl.BlockSpec((D_LORA, C_RWKV), lambda b, c: (0, 0)),
            vec(C_RWKV), vec(C_RWKV), vec(C_RWKV), vec(C_RWKV), vec(C_RWKV),
        ],
        out_specs=[
            pl.BlockSpec((L, C_RWKV), row),
            pl.BlockSpec((1, H_RWKV, RWKV_HEAD, RWKV_HEAD), lambda b, c: (b, 0, 0, 0)),
        ],
        out_shape=[
            jax.ShapeDtypeStruct((bsz * t, C_RWKV), BF16),
            jax.ShapeDtypeStruct((bsz, H_RWKV, RWKV_HEAD, RWKV_HEAD), F32),
        ],
        scratch_shapes=[
            full((N_PAIR, LANES, LANES)), full((1, D_SHIFT)),
        ],
        compiler_params=_cparams(("parallel", "arbitrary")),
        name="rwkv_mix",
    )(sh, rest, s0, prev0, mu, wd0, wd2, a0, wa2, k_k, k_a, r_k, lnw, lnb)


def _stack_maps(q):
    lane = lax.broadcasted_iota(jnp.int32, q.shape, 1)
    m0 = lane < DK_DIFF
    return jnp.concatenate([jnp.where(m0, q, 0.0), jnp.where(m0, 0.0, q)], axis=0).astype(BF16)


def _fold_lanes(x, op):
    out = x[:, 0:LANES]
    for c in range(1, x.shape[1] // LANES):
        out = op(out, x[:, c * LANES:(c + 1) * LANES])
    return out


def _softmax_tile(s_ref, p_ref, mx_ref, alpha_ref, m_ref, l_ref, bias_rows):
    nrows, tk = s_ref.shape
    ngroups = nrows // ROW_GROUP

    def group(r):
        r0 = r * ROW_GROUP
        return r0, pl.ds(r0, ROW_GROUP)

    def lane_max(r, carry):
        r0, rows = group(r)
        mx_ref[rows, :] = _fold_lanes(s_ref[rows, :] + bias_rows(r0), jnp.maximum)
        return carry

    for r in range(ngroups):
        lane_max(r, 0)
    m_prev = m_ref[...]
    m_new = jnp.maximum(m_prev, jnp.max(mx_ref[...], axis=-1, keepdims=True))
    alpha_ref[...] = jnp.exp2(m_prev - m_new)
    m_ref[...] = m_new

    def probs(r, carry):
        r0, rows = group(r)
        s = s_ref[rows, :] + bias_rows(r0)
        p = jnp.exp2(s - jnp.concatenate([m_ref[rows, :]] * (tk // LANES), axis=1))
        l_ref[rows, :] = alpha_ref[rows, :] * l_ref[rows, :] + _fold_lanes(p, jnp.add)
        p_ref[rows, :] = p.astype(BF16)
        return carry

    for r in range(ngroups):
        probs(r, 0)


def _softmax_reset(m_ref, l_ref, acc_ref):
    m_ref[...] = jnp.full(m_ref.shape, NEG_INF, F32)
    l_ref[...] = jnp.zeros(l_ref.shape, F32)
    acc_ref[...] = jnp.zeros(acc_ref.shape, F32)


def _diff_finish(acc, l, lam, lam_init, subln_g, gate):
    t = acc.shape[0] // 2
    o = acc[:t] / l[:t] - lam * (acc[t:] / l[t:])
    o = o * lax.rsqrt(jnp.mean(o * o, axis=-1, keepdims=True) + SUBLN_EPS) * subln_g * (1.0 - lam_init)
    return o * (gate * _sigmoid(gate))


def _lambda(lq1_ref, lk1_ref, lq2_ref, lk2_ref, lam_init):
    s1 = jnp.sum(lq1_ref[...] * lk1_ref[...], axis=-1, keepdims=True)
    s2 = jnp.sum(lq2_ref[...] * lk2_ref[...], axis=-1, keepdims=True)
    return jnp.exp(s1) - jnp.exp(s2) + lam_init


def _local_bias(slope, nrows, nk, tq):
    i = lax.broadcasted_iota(jnp.int32, (nrows, nk), 0)
    i = jnp.where(i >= tq, i - tq, i)
    j = lax.broadcasted_iota(jnp.int32, (nrows, nk), 1)
    bias = slope * (i - jnp.abs(i - j)).astype(F32)
    shift = CHUNK.bit_length() - 1
    visible = jnp.right_shift(j, shift) <= jnp.right_shift(i, shift)
    return jnp.where(visible, bias, NEG_INF)


def _head_slope(h):
    return lax.bitcast_convert_type(jnp.full((1, 1), (126 - h) << 23, jnp.int32), F32) * LOG2E


def _attn_prompt_kernel(qi_ref, kj_ref, q_ref, k_ref, v_ref, gd_ref, lq1_ref, lk1_ref, lq2_ref, lk2_ref, sg_ref,
                        o_ref, q2_ref, s_ref, p_ref, bias_ref, mx_ref, alpha_ref, m_ref, l_ref, acc_ref,
                        *, tq, tk, lam_init):
    h = pl.program_id(0)
    qi = qi_ref[pl.program_id(1)]
    kj = kj_ref[pl.program_id(1)]
    slope = _head_slope(h)

    @pl.when((qi == 0) & (kj == 0))
    def _():
        bias_ref[...] = _local_bias(slope, tq, tk, tq)

    @pl.when(kj == 0)
    def _():
        q2_ref[...] = _stack_maps(q_ref[...] * (DK_DIFF ** -0.5 * LOG2E))
        _softmax_reset(m_ref, l_ref, acc_ref)

    def update(bias_rows):
        s_ref[...] = _dot(q2_ref[...], k_ref[...].astype(BF16), _NT)
        _softmax_tile(s_ref, p_ref, mx_ref, alpha_ref, m_ref, l_ref, bias_rows)
        acc_ref[...] = alpha_ref[...] * acc_ref[...] + _dot(p_ref[...], v_ref[...].astype(BF16))

    @pl.when(kj < qi)
    def _():
        j = lax.broadcasted_iota(jnp.int32, (1, tk), 1)
        bias = slope * (j - (qi - kj) * tq).astype(F32)
        update(lambda r0: bias)

    @pl.when(kj == qi)
    def _():
        update(lambda r0: bias_ref[pl.ds(r0 % tq, ROW_GROUP), :])
        lam = _lambda(lq1_ref, lk1_ref, lq2_ref, lk2_ref, lam_init)
        l = jnp.sum(l_ref[...], axis=-1, keepdims=True)
        o_ref[...] = _diff_finish(acc_ref[...], l, lam, lam_init, sg_ref[...], gd_ref[...]).astype(o_ref.dtype)


def _attn_prompt(gq, k_all, v_all, layer, lq1, lk1, lq2, lk2, subln_g, lam_init, tq):
    t = gq.shape[0]
    assert t % tq == 0 and tq % CHUNK == 0
    nq = t // tq
    qcol, gcol = C_RWKV // DV_DIFF, (C_RWKV + C_DIFF) // DV_DIFF
    past = lambda: pl.BlockSpec((None, tq, DV_DIFF), lambda h, s, qi, kj: (layer, kj[s], h))
    vec = lambda n: pl.BlockSpec((1, n), lambda h, s, qi, kj: (0, 0))
    kern = functools.partial(_attn_prompt_kernel, tq=tq, tk=tq, lam_init=lam_init)
    qi_tab = jnp.asarray([i for i in range(nq) for _ in range(i + 1)], jnp.int32)
    kj_tab = jnp.asarray([j for i in range(nq) for j in range(i + 1)], jnp.int32)
    grid_spec = pltpu.PrefetchScalarGridSpec(
        num_scalar_prefetch=2,
        grid=(H_DIFF, nq * (nq + 1) // 2),
        in_specs=[
            pl.BlockSpec((tq, DV_DIFF), lambda h, s, qi, kj: (qi[s], qcol + h)),
            past(), past(),
            pl.BlockSpec((tq, DV_DIFF), lambda h, s, qi, kj: (qi[s], gcol + h)),
            vec(DK_DIFF), vec(DK_DIFF), vec(DK_DIFF), vec(DK_DIFF), vec(DV_DIFF),
        ],
        out_specs=pl.BlockSpec((tq, DV_DIFF), lambda h, s, qi, kj: (qi[s], h)),
        scratch_shapes=[
            pltpu.VMEM((2 * tq, DV_DIFF), BF16),
            pltpu.VMEM((2 * tq, tq), F32),
            pltpu.VMEM((2 * tq, tq), BF16),
            pltpu.VMEM((tq, tq), F32),
            pltpu.VMEM((2 * tq, LANES), F32), pltpu.VMEM((2 * tq, LANES), F32),
            pltpu.VMEM((2 * tq, LANES), F32), pltpu.VMEM((2 * tq, LANES), F32),
            pltpu.VMEM((2 * tq, DV_DIFF), F32),
        ],
    )
    return pl.pallas_call(
        kern,
        grid_spec=grid_spec,
        out_shape=jax.ShapeDtypeStruct((t, C_DIFF), BF16),
        compiler_params=_cparams(("arbitrary", "arbitrary")),
        name="attn_prompt",
    )(qi_tab, kj_tab, gq, k_all, v_all, gq, lq1, lk1, lq2, lk2, subln_g)


def _attn_sample_kernel(q_ref, kn_ref, vn_ref, gd_ref, ck_ref, cv_ref, lq1_ref, lk1_ref, lq2_ref, lk2_ref, sg_ref,
                        o_ref, q2_ref, s_ref, p_ref, mx_ref, alpha_ref, m_ref, l_ref, acc_ref,
                        *, t, tk, past_len, lam_init):
    kj = pl.program_id(1)
    nk = pl.num_programs(1) - 1
    heads = [(h, slice(h * DV_DIFF, (h + 1) * DV_DIFF), 2.0 ** -(h + 1) * LOG2E) for h in range(H_DIFF)]

    @pl.when(kj == 0)
    def _():
        for h, cols, _ in heads:
            q2_ref[h] = _stack_maps(q_ref[:, cols] * (DK_DIFF ** -0.5 * LOG2E))
        _softmax_reset(m_ref, l_ref, acc_ref)

    @pl.when(kj < nk)
    def _():
        j = lax.broadcasted_iota(jnp.int32, (1, tk), 1)
        dist = (j + kj * tk - past_len).astype(F32)
        for h, cols, slope in heads:
            head_rows = pl.ds(h, tk, stride=H_DIFF)
            bias = slope * dist
            s_ref[h] = _dot(q2_ref[h], ck_ref[head_rows, :].astype(BF16), _NT)
            _softmax_tile(s_ref.at[h], p_ref.at[h], mx_ref.at[h], alpha_ref.at[h], m_ref.at[h], l_ref.at[h],
                          lambda r0: bias)
            acc_ref[h] = alpha_ref[h] * acc_ref[h] + _dot(p_ref[h], cv_ref[head_rows, :].astype(BF16))

    @pl.when(kj == nk)
    def _():
        lam = _lambda(lq1_ref, lk1_ref, lq2_ref, lk2_ref, lam_init)
        for h, cols, slope in heads:
            s = _dot(q2_ref[h], kn_ref[:, cols].astype(BF16), _NT) + _local_bias(slope, 2 * t, t, t)
            m_prev = m_ref[h]
            m_new = jnp.maximum(m_prev, jnp.max(s, axis=-1, keepdims=True))
            alpha = jnp.exp2(m_prev - m_new)
            p = jnp.exp2(s - m_new[:, 0:t])
            l = jnp.sum(alpha * l_ref[h], axis=-1, keepdims=True) + jnp.sum(p, axis=-1, keepdims=True)
            acc = alpha * acc_ref[h] + _dot(p.astype(BF16), vn_ref[:, cols].astype(BF16))
            o_ref[:, cols] = _diff_finish(acc, l, lam, lam_init, sg_ref[...], gd_ref[:, cols]).astype(o_ref.dtype)


def _attn_sample(gq, k_all, v_all, cache_k, cache_v, layer, lq1, lk1, lq2, lk2, subln_g, lam_init, tk):
    depth, bsz, past_len = cache_k.shape[:3]
    cache_k = cache_k.reshape(depth, bsz, past_len * H_DIFF, DV_DIFF)
    cache_v = cache_v.reshape(depth, bsz, past_len * H_DIFF, DV_DIFF)
    t = gq.shape[0] // bsz
    assert t <= CHUNK and past_len % CHUNK == 0 and past_len % tk == 0
    nk = past_len // tk
    vec = lambda n: pl.BlockSpec((1, n), lambda b, j: (0, 0))
    new = lambda col: pl.BlockSpec((t, C_DIFF), lambda b, j: (b, col))
    fresh = pl.BlockSpec((None, t, C_DIFF), lambda b, j: (layer, b, 0))
    past = pl.BlockSpec((None, None, tk * H_DIFF, DV_DIFF),
                        lambda b, j: (layer, b, jnp.minimum(j, nk - 1), 0))
    kern = functools.partial(_attn_sample_kernel, t=t, tk=tk, past_len=past_len, lam_init=lam_init)
    per_head = lambda shape, dtype: pltpu.VMEM((H_DIFF,) + shape, dtype)
    return pl.pallas_call(
        kern,
        grid=(bsz, nk + 1),
        in_specs=[new(1), fresh, fresh, new(2), past, past,
                  vec(DK_DIFF), vec(DK_DIFF), vec(DK_DIFF), vec(DK_DIFF), vec(DV_DIFF)],
        out_specs=pl.BlockSpec((t, C_DIFF), lambda b, j: (b, 0)),
        out_shape=jax.ShapeDtypeStruct((bsz * t, C_DIFF), BF16),
        scratch_shapes=[
            per_head((2 * t, DV_DIFF), BF16),
            per_head((2 * t, tk), F32),
            per_head((2 * t, tk), BF16),
            per_head((2 * t, LANES), F32), per_head((2 * t, LANES), F32),
            per_head((2 * t, LANES), F32), per_head((2 * t, LANES), F32),
            per_head((2 * t, DV_DIFF), F32),
        ],
        compiler_params=_cparams(("parallel", "arbitrary")),
        name="attn_sample",
    )(gq, k_all, v_all, gq, cache_k, cache_v, lq1, lk1, lq2, lk2, subln_g)


def kernel(x_prompt, x_sample, cache_k, cache_v, state_wkv, state_shift, norm_g, w_in, shift_mu, w_decay0, w_decay2,
           a0, w_a2, k_k, k_a, r_k, lnx_w, lnx_b, lam_q1, lam_k1, lam_q2, lam_k2, subln_g, w_out, final_g):
    depth = w_in.shape[0]
    bp, tp, d = x_prompt.shape
    bs, ts, _ = x_sample.shape
    assert bp == 1, "the prompt attention kernel handles one sequence"

    xp = x_prompt.reshape(bp * tp, d)
    xs = x_sample.reshape(bs * ts, d)
    w_in_b = w_in.astype(BF16)
    w_out_b = w_out.astype(BF16)
    zero_state = jnp.zeros((bp, H_RWKV, RWKV_HEAD, RWKV_HEAD), state_wkv.dtype)
    zero_shift = jnp.zeros((bp, 1, D_SHIFT), state_shift.dtype)
    row = lambda a: a.reshape(1, -1)

    outs = {name: [] for name in ("sp", "hp", "ss", "hs")}
    kv_p = [jnp.zeros((depth, bp * tp, C_DIFF), F32) for _ in range(2)]
    kv_s = [jnp.zeros((depth, bs * ts, C_DIFF), F32) for _ in range(2)]
    for l in range(depth):
        lam_init = _lambda_init(l)
        g = row(norm_g[l])
        w_sh = w_in_b[l, :, :D_SHIFT]
        w_rest = w_in_b[l, :, D_SHIFT:]
        rw = (row(shift_mu[l]), row(w_decay0[l]), w_decay2[l], row(a0[l]), w_a2[l], row(k_k[l]), row(k_a[l]),
              row(r_k[l]), row(lnx_w[l]), row(lnx_b[l]))
        lam = (row(lam_q1[l]), row(lam_k1[l]), row(lam_q2[l]), row(lam_k2[l]), row(subln_g[l]))

        def stream(x, bsz, kv, s0, prev0, attn):
            sh = _norm_matmul(x, g, w_sh, 1024, 640)
            gq, kv[0], kv[1] = _proj_rest(x, g, w_rest, kv[0], kv[1], l, 512)
            y_r, s_t = _rwkv_mix(sh, gq, s0, prev0, *rw)
            y_d = attn(gq, kv[0], kv[1])
            x = _out_matmul(x, y_r, y_d, w_out_b[l], 512, 1024)
            shift = sh.reshape(bsz, sh.shape[0] // bsz, D_SHIFT)[:, -1]
            return x, s_t, shift

        xp, sn, hn = stream(
            xp, bp, kv_p, zero_state, zero_shift,
            lambda gq, k_all, v_all: _attn_prompt(gq, k_all, v_all, l, *lam, lam_init, 512))
        outs["sp"].append(sn); outs["hp"].append(hn)
        xs, sn, hn = stream(
            xs, bs, kv_s, state_wkv[l], state_shift[l].reshape(bs, 1, D_SHIFT),
            lambda gq, k_all, v_all: _attn_sample(gq, k_all, v_all, cache_k, cache_v, l, *lam, lam_init, 1024))
        outs["ss"].append(sn); outs["hs"].append(hn)

    yp = _final_norm(xp, row(final_g), 512).reshape(bp, tp, d)
    ys = _final_norm(xs, row(final_g), 512).reshape(bs, ts, d)
    st = lambda name: jnp.stack(outs[name])
    heads = lambda a, bsz, t: a.reshape(depth, bsz, t, H_DIFF, DV_DIFF)
    return (yp, ys, heads(kv_p[0], bp, tp), heads(kv_p[1], bp, tp), st("sp"), st("hp"),
            heads(kv_s[0], bs, ts), heads(kv_s[1], bs, ts), st("ss"), st("hs"))
```

```python
import functools
import math

import jax
import jax.numpy as jnp
from jax import lax
from jax.experimental import pallas as pl
from jax.experimental.pallas import tpu as pltpu

F32 = jnp.float32
BF16 = jnp.bfloat16

D_MODEL = 2048
CHUNK = 64
C_RWKV = 1024
RWKV_HEAD = 64
H_RWKV = C_RWKV // RWKV_HEAD
N_PAIR = H_RWKV // 2
D_LORA = 64
D_SHIFT = 3 * C_RWKV + 2 * D_LORA
C_DIFF = 1024
H_DIFF = 8
DV_DIFF = 128
DK_DIFF = 64
D_REST = C_RWKV + 4 * C_DIFF
NORM_EPS = 1e-6
RWKV_LN_EPS = 64e-5
SUBLN_EPS = 1e-5
NEG_INF = -1e30
LANES = 128
L_CHUNK = 64
EXP_M05 = math.exp(-0.5)
LOG2E = math.log2(math.e)
ROW_GROUP = 16
VMEM_LIMIT = 48 * 1024 * 1024


def _lambda_init(l):
    return 0.8 - 0.6 * math.exp(-0.3 * l)


def _cparams(sem, vmem_limit=VMEM_LIMIT):
    return pltpu.CompilerParams(dimension_semantics=sem, vmem_limit_bytes=vmem_limit)


_NN = (((1,), (0,)), ((), ()))
_NT = (((1,), (1,)), ((), ()))
_TN = (((0,), (0,)), ((), ()))
_BNN = (((2,), (1,)), ((0,), (0,)))
_BNT = (((2,), (2,)), ((0,), (0,)))
_BTN = (((1,), (1,)), ((0,), (0,)))


def _dot(a, b, dims=_NN):
    return lax.dot_general(a, b, dims, preferred_element_type=F32)


def _split2(x):
    hi = x.astype(BF16)
    lo = (x - hi.astype(F32)).astype(BF16)
    return hi, lo


def _split3(x):
    hi = x.astype(BF16)
    r1 = x - hi.astype(F32)
    mid = r1.astype(BF16)
    lo = (r1 - mid.astype(F32)).astype(BF16)
    return hi, mid, lo


def _mm3(a, b, dims=_NN):
    ah, al = _split2(a)
    bh, bl = _split2(b)
    return _dot(ah, bh, dims) + (_dot(ah, bl, dims) + _dot(al, bh, dims))


def _mm_exact_lhs(a_bf16, b, dims=_NN):
    bh, bm, bl = _split3(b)
    return _dot(a_bf16, bh, dims) + (_dot(a_bf16, bm, dims) + _dot(a_bf16, bl, dims))


def _sum_heads(a, e2):
    return _dot(a.astype(BF16), e2)


def _sigmoid(x):
    return 1.0 / (1.0 + jnp.exp(-x))


def _norm_matmul_kernel(x_ref, g_ref, w_ref, o_ref, h_ref):
    @pl.when(pl.program_id(1) == 0)
    def _():
        x = x_ref[...]
        ms = jnp.mean(x * x, axis=-1, keepdims=True)
        h_ref[...] = (x * lax.rsqrt(ms + NORM_EPS) * g_ref[...]).astype(BF16)

    o_ref[...] = jnp.dot(h_ref[...], w_ref[...], preferred_element_type=F32)


def _norm_matmul(x, g, w, tm, tn):
    m, d = x.shape
    n = w.shape[1]
    return pl.pallas_call(
        _norm_matmul_kernel,
        grid=(m // tm, n // tn),
        in_specs=[
            pl.BlockSpec((tm, d), lambda i, j: (i, 0)),
            pl.BlockSpec((1, d), lambda i, j: (0, 0)),
            pl.BlockSpec((d, tn), lambda i, j: (0, j)),
        ],
        out_specs=pl.BlockSpec((tm, tn), lambda i, j: (i, j)),
        out_shape=jax.ShapeDtypeStruct((m, n), F32),
        scratch_shapes=[pltpu.VMEM((tm, d), BF16)],
        compiler_params=_cparams(("parallel", "arbitrary")),
        name="norm_matmul",
    )(x, g, w)


def _proj_rest_kernel(x_ref, g_ref, w_ref, kin_ref, vin_ref, o_ref, k_ref, v_ref, k16_ref, v16_ref, h_ref):
    del kin_ref, vin_ref
    j = pl.program_id(1)

    @pl.when(j == 0)
    def _():
        x = x_ref[...]
        ms = jnp.mean(x * x, axis=-1, keepdims=True)
        h_ref[...] = (x * lax.rsqrt(ms + NORM_EPS) * g_ref[...]).astype(BF16)

    acc = jnp.dot(h_ref[...], w_ref[...], preferred_element_type=F32)

    @pl.when((j != _KCOL) & (j != _VCOL))
    def _():
        o_ref[...] = acc

    def by_head(ref16):
        for h in range(H_DIFF):
            ref16[h] = acc[:, h * DV_DIFF:(h + 1) * DV_DIFF].astype(BF16)

    @pl.when(j == _KCOL)
    def _():
        k_ref[...] = acc
        by_head(k16_ref)

    @pl.when(j == _VCOL)
    def _():
        v_ref[...] = acc
        by_head(v16_ref)


_KCOL, _VCOL = 2, 3


def _proj_rest(x, g, w, k_all, v_all, layer, tm):
    m, d = x.shape
    tn = C_DIFF
    assert w.shape[1] == 5 * tn and C_RWKV == tn
    ocol = lambda j: jnp.minimum(j, 1) + j // 4
    return pl.pallas_call(
        _proj_rest_kernel,
        grid=(m // tm, 5),
        in_specs=[
            pl.BlockSpec((tm, d), lambda i, j: (i, 0)),
            pl.BlockSpec((1, d), lambda i, j: (0, 0)),
            pl.BlockSpec((d, tn), lambda i, j: (0, j)),
            pl.BlockSpec(memory_space=pl.ANY),
            pl.BlockSpec(memory_space=pl.ANY),
        ],
        out_specs=[
            pl.BlockSpec((tm, tn), lambda i, j: (i, ocol(j))),
            pl.BlockSpec((None, tm, tn), lambda i, j: (layer, i, 0)),
            pl.BlockSpec((None, tm, tn), lambda i, j: (layer, i, 0)),
            pl.BlockSpec((H_DIFF, tm, DV_DIFF), lambda i, j: (0, i, 0)),
            pl.BlockSpec((H_DIFF, tm, DV_DIFF), lambda i, j: (0, i, 0)),
        ],
        out_shape=[
            jax.ShapeDtypeStruct((m, 3 * tn), F32),
            jax.ShapeDtypeStruct(k_all.shape, F32),
            jax.ShapeDtypeStruct(v_all.shape, F32),
            jax.ShapeDtypeStruct((H_DIFF, m, DV_DIFF), BF16),
            jax.ShapeDtypeStruct((H_DIFF, m, DV_DIFF), BF16),
        ],
        input_output_aliases={3: 1, 4: 2},
        scratch_shapes=[pltpu.VMEM((tm, d), BF16)],
        compiler_params=_cparams(("parallel", "arbitrary"),
                                 8 * tm * d + 4 * d * tn + 32 * tm * tn + 2 * tm * d + (4 << 20)),
        name="proj_rest",
    )(x, g, w, k_all, v_all)


def _final_norm_kernel(x_ref, g_ref, o_ref):
    x = x_ref[...]
    ms = jnp.mean(x * x, axis=-1, keepdims=True)
    o_ref[...] = x * lax.rsqrt(ms + NORM_EPS) * g_ref[...]


def _final_norm(x, g, tm):
    m, d = x.shape
    return pl.pallas_call(
        _final_norm_kernel,
        grid=(m // tm,),
        in_specs=[pl.BlockSpec((tm, d), lambda i: (i, 0)), pl.BlockSpec((1, d), lambda i: (0, 0))],
        out_specs=pl.BlockSpec((tm, d), lambda i: (i, 0)),
        out_shape=jax.ShapeDtypeStruct((m, d), F32),
        compiler_params=_cparams(("parallel",)),
        name="final_norm",
    )(x, g)


def _out_matmul_kernel(x_ref, yr_ref, yd_ref, wr_ref, wd_ref, o_ref):
    acc = jnp.dot(yr_ref[...], wr_ref[...], preferred_element_type=F32)
    acc = acc + jnp.dot(yd_ref[...], wd_ref[...], preferred_element_type=F32)
    o_ref[...] = x_ref[...] + acc


def _out_matmul(x, y_r, y_d, w_out, tm, tn):
    m, d = x.shape
    return pl.pallas_call(
        _out_matmul_kernel,
        grid=(m // tm, d // tn),
        in_specs=[
            pl.BlockSpec((tm, tn), lambda i, j: (i, j)),
            pl.BlockSpec((tm, C_RWKV), lambda i, j: (i, 0)),
            pl.BlockSpec((tm, C_DIFF), lambda i, j: (i, 0)),
            pl.BlockSpec((C_RWKV, tn), lambda i, j: (0, j)),
            pl.BlockSpec((C_DIFF, tn), lambda i, j: (1, j)),
        ],
        out_specs=pl.BlockSpec((tm, tn), lambda i, j: (i, j)),
        out_shape=jax.ShapeDtypeStruct((m, d), F32),
        compiler_params=_cparams(("parallel", "parallel")),
        name="out_matmul",
    )(x, y_r, y_d, w_out, w_out)


def _rwkv_kernel(sh_ref, gr_ref, s0_ref, p0_ref, mu_ref, wd0_ref, wd2_ref, a0_ref, wa2_ref,
                 kk_ref, ka_ref, rk_ref, lnw_ref, lnb_ref,
                 y_ref, st_ref,
                 bd_ref, prev_ref):
    c = pl.program_id(1)
    nc = pl.num_programs(1)
    L = L_CHUNK
    C = C_RWKV

    @pl.when(c == 0)
    def _():
        prev_ref[...] = p0_ref[0]
        for p in range(N_PAIR):
            bd_ref[p] = jnp.zeros((LANES, LANES), F32)
            bd_ref[p, 0:RWKV_HEAD, 0:RWKV_HEAD] = s0_ref[0, 2 * p]
            bd_ref[p, RWKV_HEAD:LANES, RWKV_HEAD:LANES] = s0_ref[0, 2 * p + 1]

    sh = sh_ref[...]
    row0 = lax.broadcasted_iota(jnp.int32, sh.shape, 0) == 0
    prev = jnp.where(row0, prev_ref[...], pltpu.roll(sh, 1, 0))
    xs = sh + (prev - sh) * mu_ref[...]
    prev_ref[...] = sh_ref[L - 1:L, :]

    r = xs[:, 0:C]
    kr = xs[:, C:2 * C]
    v = xs[:, 2 * C:3 * C]
    la = xs[:, 3 * C:3 * C + LANES]

    lane = lax.broadcasted_iota(jnp.int32, (L, LANES), 1)
    head0 = lane < RWKV_HEAD
    w2 = jnp.concatenate([wd2_ref[...], wa2_ref[...]], axis=0)
    dw = _mm3(jnp.where(head0, jnp.tanh(la), 0.0), w2)
    da = _mm3(jnp.where(head0, 0.0, la), w2)
    logw = -EXP_M05 * _sigmoid(wd0_ref[...] + dw)
    a = _sigmoid(a0_ref[...] + da)

    er = lax.broadcasted_iota(jnp.int32, (LANES, LANES), 0)
    ec = lax.broadcasted_iota(jnp.int32, (LANES, LANES), 1)
    e2 = jnp.where((er < RWKV_HEAD) == (ec < RWKV_HEAD), 1.0, 0.0).astype(BF16)

    def head_sums(x):
        return jnp.concatenate(
            [_sum_heads(x[:, p * LANES:(p + 1) * LANES], e2) for p in range(N_PAIR)], axis=1)

    kk = kr * kk_ref[...]
    kk = kk / jnp.maximum(jnp.sqrt(head_sums(kk * kk)), 1e-12)
    kw = kr * (1.0 + (a - 1.0) * ka_ref[...])
    b = kk * a

    tr = lax.broadcasted_iota(jnp.int32, (L, L), 0)
    tc = lax.broadcasted_iota(jnp.int32, (L, L), 1)
    tril = jnp.where(tr >= tc, 1.0, 0.0).astype(BF16)
    cum = _mm_exact_lhs(tril, logw)
    cum_l = cum[L - 1:L, :]
    gi = jnp.exp(-cum)
    gl = jnp.exp(cum_l - cum)
    strict = er > ec
    incl = er >= ec
    eye = jnp.where(er == ec, 1.0, 0.0)

    def blockdiag(x):
        tiles = []
        for p in range(N_PAIR):
            xp = x[:, p * LANES:(p + 1) * LANES]
            tiles.append(jnp.concatenate([jnp.where(head0, xp, 0.0), jnp.where(head0, 0.0, xp)], axis=0))
        return jnp.stack(tiles).astype(BF16)

    kt = blockdiag(kk * jnp.exp(cum - logw))
    rt = blockdiag(r * jnp.exp(cum))
    ki = blockdiag(kw * gi)
    bi = blockdiag(b * gi)
    kh = blockdiag(kw * gl)
    bh = blockdiag(b * gl)
    vb = blockdiag(v)
    lhs = jnp.concatenate([kt, rt], axis=1)
    gram = _dot(lhs, jnp.concatenate([ki, bi], axis=1), _BNT)
    a_kk = jnp.where(strict, gram[:, :2 * L, :2 * L], 0.0)
    a_kb = jnp.where(strict, gram[:, :2 * L, 2 * L:], 0.0)
    a_rk = jnp.where(incl, gram[:, 2 * L:, :2 * L], 0.0)
    a_rb = jnp.where(incl, gram[:, 2 * L:, 2 * L:], 0.0)
    t = eye - a_kb
    qb = (-a_kb).astype(BF16)
    qb = _dot(qb, qb, _BNN).astype(BF16)
    for _ in range(4):
        both = _dot(qb, jnp.concatenate([qb, t.astype(BF16)], axis=2), _BNN)
        qb = both[:, :, :2 * L].astype(BF16)
        t = t + both[:, :, 2 * L:]
    t = t + _dot(qb, t.astype(BF16), _BNN)
    s0 = bd_ref[...]
    zy = _dot(lhs, s0.astype(BF16), _BNT)
    z = zy[:, :2 * L] + _dot(a_kk.astype(BF16), vb, _BNN)
    u = _dot(t.astype(BF16), z.astype(BF16), _BNN)
    vu = jnp.concatenate([vb, u.astype(BF16)], axis=1)
    yb = zy[:, 2 * L:] + _dot(jnp.concatenate([a_rk, -a_rb], axis=2).astype(BF16), vu, _BNN)
    g_l = jnp.exp(cum_l)
    decay = jnp.stack([g_l[:, p * LANES:(p + 1) * LANES] for p in range(N_PAIR)])
    bd_ref[...] = s0 * decay + _dot(vu, jnp.concatenate([kh, -bh], axis=1), _BTN)
    y = jnp.concatenate([yb[p, :L] + yb[p, L:] for p in range(N_PAIR)], axis=1)

    d = y - head_sums(y) * (1.0 / RWKV_HEAD)
    var = head_sums(d * d) * (1.0 / RWKV_HEAD)
    yn = d * lax.rsqrt(var + RWKV_LN_EPS) * lnw_ref[...] + lnb_ref[...]
    bonus = head_sums(r * kw * rk_ref[...]) * v
    g = gr_ref[...]
    y_ref[...] = ((yn + bonus) * (g * _sigmoid(g))).astype(y_ref.dtype)

    @pl.when(c == nc - 1)
    def _():
        for p in range(N_PAIR):
            st_ref[0, 2 * p] = bd_ref[p, 0:RWKV_HEAD, 0:RWKV_HEAD]
            st_ref[0, 2 * p + 1] = bd_ref[p, RWKV_HEAD:LANES, RWKV_HEAD:LANES]


def _rwkv_mix(sh, rest, s0, prev0, mu, wd0, wd2, a0, wa2, k_k, k_a, r_k, lnw, lnb):
    bsz = s0.shape[0]
    t = sh.shape[0] // bsz
    assert t % L_CHUNK == 0
    nc = t // L_CHUNK
    L = L_CHUNK
    row = lambda b, c: (b * nc + c, 0)
    vec = lambda n: pl.BlockSpec((1, n), lambda b, c: (0, 0))
    full = lambda shape: pltpu.VMEM(shape, F32)
    return pl.pallas_call(
        _rwkv_kernel,
        grid=(bsz, nc),
        in_specs=[
            pl.BlockSpec((L, D_SHIFT), row),
            pl.BlockSpec((L, C_RWKV), row),
            pl.BlockSpec((1, H_RWKV, RWKV_HEAD, RWKV_HEAD), lambda b, c: (b, 0, 0, 0)),
            pl.BlockSpec((1, 1, D_SHIFT), lambda b, c: (b, 0, 0)),
            vec(D_SHIFT), vec(C_RWKV),
            pl.BlockSpec((D_LORA, C_RWKV), lambda b, c: (0, 0)),
            vec(C_RWKV),
            pl.BlockSpec((D_LORA, C_RWKV), lambda b, c: (0, 0)),
            vec(C_RWKV), vec(C_RWKV), vec(C_RWKV), vec(C_RWKV), vec(C_RWKV),
        ],
        out_specs=[
            pl.BlockSpec((L, C_RWKV), row),
            pl.BlockSpec((1, H_RWKV, RWKV_HEAD, RWKV_HEAD), lambda b, c: (b, 0, 0, 0)),
        ],
        out_shape=[
            jax.ShapeDtypeStruct((bsz * t, C_RWKV), BF16),
            jax.ShapeDtypeStruct((bsz, H_RWKV, RWKV_HEAD, RWKV_HEAD), F32),
        ],
        scratch_shapes=[
            full((N_PAIR, LANES, LANES)), full((1, D_SHIFT)),
        ],
        compiler_params=_cparams(("parallel", "arbitrary")),
        name="rwkv_mix",
    )(sh, rest, s0, prev0, mu, wd0, wd2, a0, wa2, k_k, k_a, r_k, lnw, lnb)


def _stack_maps(q):
    lane = lax.broadcasted_iota(jnp.int32, q.shape, 1)
    m0 = lane < DK_DIFF
    return jnp.concatenate([jnp.where(m0, q, 0.0), jnp.where(m0, 0.0, q)], axis=0).astype(BF16)


def _fold_lanes(x, op):
    out = x[:, 0:LANES]
    for c in range(1, x.shape[1] // LANES):
        out = op(out, x[:, c * LANES:(c + 1) * LANES])
    return out


def _softmax_tile(s_ref, p_ref, mx_ref, alpha_ref, m_ref, bias_rows, l_ref=None):
    nrows, tk = s_ref.shape
    for r0 in range(0, nrows, ROW_GROUP):
        rows = pl.ds(r0, ROW_GROUP)
        mx_ref[rows, :] = _fold_lanes(s_ref[rows, :] + bias_rows(r0), jnp.maximum)
    m_prev = m_ref[...]
    m_new = jnp.maximum(m_prev, jnp.max(mx_ref[...], axis=-1, keepdims=True))
    alpha_ref[...] = jnp.exp2(m_prev - m_new)
    m_ref[...] = m_new
    for r0 in range(0, nrows, ROW_GROUP):
        rows = pl.ds(r0, ROW_GROUP)
        s = s_ref[rows, :] + bias_rows(r0)
        p = jnp.exp2(s - jnp.concatenate([m_ref[rows, :]] * (tk // LANES), axis=1))
        if l_ref is not None:
            l_ref[rows, :] = alpha_ref[rows, :] * l_ref[rows, :] + _fold_lanes(p, jnp.add)
        p_ref[rows, :] = p.astype(BF16)


def _values_and_ones(v):
    return jnp.concatenate([v.astype(BF16), jnp.ones(v.shape, BF16)], axis=1)


def _rescale(alpha):
    return jnp.concatenate([alpha, alpha], axis=1)


def _softmax_reset(m_ref, acc_ref):
    m_ref[...] = jnp.full(m_ref.shape, NEG_INF, F32)
    acc_ref[...] = jnp.zeros(acc_ref.shape, F32)


def _diff_finish(vals, sums, lam, lam_init, subln_g, gate):
    t = vals.shape[0] // 2
    o = vals / sums
    o = o[:t] - lam * o[t:]
    o = o * lax.rsqrt(jnp.mean(o * o, axis=-1, keepdims=True) + SUBLN_EPS) * subln_g * (1.0 - lam_init)
    return o * (gate * _sigmoid(gate))


def _lambda(lq1_ref, lk1_ref, lq2_ref, lk2_ref, lam_init):
    s1 = jnp.sum(lq1_ref[...] * lk1_ref[...], axis=-1, keepdims=True)
    s2 = jnp.sum(lq2_ref[...] * lk2_ref[...], axis=-1, keepdims=True)
    return jnp.exp(s1) - jnp.exp(s2) + lam_init


def _local_bias(slope, nrows, nk, tq):
    i = lax.broadcasted_iota(jnp.int32, (nrows, nk), 0)
    i = jnp.where(i >= tq, i - tq, i)
    j = lax.broadcasted_iota(jnp.int32, (nrows, nk), 1)
    bias = slope * (i - jnp.abs(i - j)).astype(F32)
    shift = CHUNK.bit_length() - 1
    visible = jnp.right_shift(j, shift) <= jnp.right_shift(i, shift)
    return jnp.where(visible, bias, NEG_INF)


def _head_slope(h):
    return lax.bitcast_convert_type(jnp.full((1, 1), (126 - h) << 23, jnp.int32), F32) * LOG2E


def _attn_prompt_kernel(qi_ref, kj_ref, q_ref, k_ref, v_ref, gd_ref, lq1_ref, lk1_ref, lq2_ref, lk2_ref, sg_ref,
                        o_ref, q2_ref, s_ref, p_ref, bias_ref, mx_ref, alpha_ref, m_ref, l_ref, acc_ref,
                        *, tq, tk, lam_init):
    h = pl.program_id(0)
    qi = qi_ref[pl.program_id(1)]
    kj = kj_ref[pl.program_id(1)]
    slope = _head_slope(h)

    @pl.when((qi == 0) & (kj == 0))
    def _():
        bias_ref[...] = _local_bias(slope, tq, tk, tq)

    @pl.when(kj == 0)
    def _():
        q2_ref[...] = _stack_maps(q_ref[...] * (DK_DIFF ** -0.5 * LOG2E))
        _softmax_reset(m_ref, acc_ref)
        l_ref[...] = jnp.zeros(l_ref.shape, F32)

    def update(bias_rows):
        s_ref[...] = _dot(q2_ref[...], k_ref[...].astype(BF16), _NT)
        _softmax_tile(s_ref, p_ref, mx_ref, alpha_ref, m_ref, bias_rows, l_ref)
        acc_ref[...] = alpha_ref[...] * acc_ref[...] + _dot(p_ref[...], v_ref[...].astype(BF16))

    @pl.when(kj < qi)
    def _():
        j = lax.broadcasted_iota(jnp.int32, (1, tk), 1)
        bias = slope * (j - (qi - kj) * tq).astype(F32)
        update(lambda r0: bias)

    @pl.when(kj == qi)
    def _():
        update(lambda r0: bias_ref[pl.ds(r0 % tq, ROW_GROUP), :])
        lam = _lambda(lq1_ref, lk1_ref, lq2_ref, lk2_ref, lam_init)
        l = jnp.sum(l_ref[...], axis=-1, keepdims=True)
        o_ref[...] = _diff_finish(acc_ref[...], l, lam, lam_init, sg_ref[...], gd_ref[...]).astype(o_ref.dtype)


def _attn_prompt(gq, k16, v16, lq1, lk1, lq2, lk2, subln_g, lam_init, tq):
    t = gq.shape[0]
    assert t % tq == 0 and tq % CHUNK == 0
    nq = t // tq
    qcol, gcol = C_RWKV // DV_DIFF, (C_RWKV + C_DIFF) // DV_DIFF
    past = lambda: pl.BlockSpec((None, tq, DV_DIFF), lambda h, s, qi, kj: (h, kj[s], 0))
    vec = lambda n: pl.BlockSpec((1, n), lambda h, s, qi, kj: (0, 0))
    kern = functools.partial(_attn_prompt_kernel, tq=tq, tk=tq, lam_init=lam_init)
    qi_tab = jnp.asarray([i for i in range(nq) for _ in range(i + 1)], jnp.int32)
    kj_tab = jnp.asarray([j for i in range(nq) for j in range(i + 1)], jnp.int32)
    grid_spec = pltpu.PrefetchScalarGridSpec(
        num_scalar_prefetch=2,
        grid=(H_DIFF, nq * (nq + 1) // 2),
        in_specs=[
            pl.BlockSpec((tq, DV_DIFF), lambda h, s, qi, kj: (qi[s], qcol + h)),
            past(), past(),
            pl.BlockSpec((tq, DV_DIFF), lambda h, s, qi, kj: (qi[s], gcol + h)),
            vec(DK_DIFF), vec(DK_DIFF), vec(DK_DIFF), vec(DK_DIFF), vec(DV_DIFF),
        ],
        out_specs=pl.BlockSpec((tq, DV_DIFF), lambda h, s, qi, kj: (qi[s], h)),
        scratch_shapes=[
            pltpu.VMEM((2 * tq, DV_DIFF), BF16),
            pltpu.VMEM((2 * tq, tq), F32),
            pltpu.VMEM((2 * tq, tq), BF16),
            pltpu.VMEM((tq, tq), F32),
            pltpu.VMEM((2 * tq, LANES), F32), pltpu.VMEM((2 * tq, LANES), F32),
            pltpu.VMEM((2 * tq, LANES), F32),
            pltpu.VMEM((2 * tq, LANES), F32),
            pltpu.VMEM((2 * tq, DV_DIFF), F32),
        ],
    )
    return pl.pallas_call(
        kern,
        grid_spec=grid_spec,
        out_shape=jax.ShapeDtypeStruct((t, C_DIFF), BF16),
        compiler_params=_cparams(("arbitrary", "arbitrary")),
        name="attn_prompt",
    )(qi_tab, kj_tab, gq, k16, v16, gq, lq1, lk1, lq2, lk2, subln_g)


def _attn_sample_kernel(q_ref, kn_ref, vn_ref, gd_ref, ck_ref, cv_ref, lq1_ref, lk1_ref, lq2_ref, lk2_ref, sg_ref,
                        o_ref, q2_ref, s_ref, p_ref, mx_ref, alpha_ref, m_ref, acc_ref,
                        *, t, tk, past_len, lam_init):
    kj = pl.program_id(1)
    nk = pl.num_programs(1) - 1
    heads = [(h, slice(h * DV_DIFF, (h + 1) * DV_DIFF), 2.0 ** -(h + 1) * LOG2E) for h in range(H_DIFF)]

    @pl.when(kj == 0)
    def _():
        for h, cols, _ in heads:
            q2_ref[h] = _stack_maps(q_ref[:, cols] * (DK_DIFF ** -0.5 * LOG2E))
        _softmax_reset(m_ref, acc_ref)

    @pl.when(kj < nk)
    def _():
        j = lax.broadcasted_iota(jnp.int32, (1, tk), 1)
        dist = (j + kj * tk - past_len).astype(F32)
        for h, cols, slope in heads:
            head_rows = pl.ds(h, tk, stride=H_DIFF)
            bias = slope * dist
            s_ref[h] = _dot(q2_ref[h], ck_ref[head_rows, :].astype(BF16), _NT)
            _softmax_tile(s_ref.at[h], p_ref.at[h], mx_ref.at[h], alpha_ref.at[h], m_ref.at[h], lambda r0: bias)
            acc_ref[h] = (_rescale(alpha_ref[h]) * acc_ref[h]
                          + _dot(p_ref[h], _values_and_ones(cv_ref[head_rows, :])))

    @pl.when(kj == nk)
    def _():
        lam = _lambda(lq1_ref, lk1_ref, lq2_ref, lk2_ref, lam_init)
        for h, cols, slope in heads:
            s = _dot(q2_ref[h], kn_ref[h], _NT) + _local_bias(slope, 2 * t, t, t)
            m_prev = m_ref[h]
            m_new = jnp.maximum(m_prev, jnp.max(s, axis=-1, keepdims=True))
            alpha = jnp.exp2(m_prev - m_new)
            p = jnp.exp2(s - m_new[:, 0:t]).astype(BF16)
            acc = _rescale(alpha) * acc_ref[h] + _dot(p, _values_and_ones(vn_ref[h]))
            o_ref[:, cols] = _diff_finish(acc[:, :DV_DIFF], acc[:, DV_DIFF:], lam, lam_init, sg_ref[...],
                                          gd_ref[:, cols]).astype(o_ref.dtype)


def _attn_sample(gq, k16, v16, cache_k, cache_v, layer, lq1, lk1, lq2, lk2, subln_g, lam_init, tk):
    depth, bsz, past_len = cache_k.shape[:3]
    cache_k = cache_k.reshape(depth, bsz, past_len * H_DIFF, DV_DIFF)
    cache_v = cache_v.reshape(depth, bsz, past_len * H_DIFF, DV_DIFF)
    t = gq.shape[0] // bsz
    assert t <= CHUNK and past_len % CHUNK == 0 and past_len % tk == 0
    nk = past_len // tk
    vec = lambda n: pl.BlockSpec((1, n), lambda b, j: (0, 0))
    new = lambda col: pl.BlockSpec((t, C_DIFF), lambda b, j: (b, col))
    fresh = pl.BlockSpec((H_DIFF, t, DV_DIFF), lambda b, j: (0, b, 0))
    past = pl.BlockSpec((None, None, tk * H_DIFF, DV_DIFF),
                        lambda b, j: (layer, b, jnp.minimum(j, nk - 1), 0))
    kern = functools.partial(_attn_sample_kernel, t=t, tk=tk, past_len=past_len, lam_init=lam_init)
    per_head = lambda shape, dtype: pltpu.VMEM((H_DIFF,) + shape, dtype)
    return pl.pallas_call(
        kern,
        grid=(bsz, nk + 1),
        in_specs=[new(1), fresh, fresh, new(2), past, past,
                  vec(DK_DIFF), vec(DK_DIFF), vec(DK_DIFF), vec(DK_DIFF), vec(DV_DIFF)],
        out_specs=pl.BlockSpec((t, C_DIFF), lambda b, j: (b, 0)),
        out_shape=jax.ShapeDtypeStruct((bsz * t, C_DIFF), BF16),
        scratch_shapes=[
            per_head((2 * t, DV_DIFF), BF16),
            per_head((2 * t, tk), F32),
            per_head((2 * t, tk), BF16),
            per_head((2 * t, LANES), F32), per_head((2 * t, LANES), F32),
            per_head((2 * t, LANES), F32),
            per_head((2 * t, 2 * DV_DIFF), F32),
        ],
        compiler_params=_cparams(("parallel", "arbitrary")),
        name="attn_sample",
    )(gq, k16, v16, gq, cache_k, cache_v, lq1, lk1, lq2, lk2, subln_g)


def kernel(x_prompt, x_sample, cache_k, cache_v, state_wkv, state_shift, norm_g, w_in, shift_mu, w_decay0, w_decay2,
           a0, w_a2, k_k, k_a, r_k, lnx_w, lnx_b, lam_q1, lam_k1, lam_q2, lam_k2, subln_g, w_out, final_g):
    depth = w_in.shape[0]
    bp, tp, d = x_prompt.shape
    bs, ts, _ = x_sample.shape
    assert bp == 1, "the prompt attention kernel handles one sequence"

    xp = x_prompt.reshape(bp * tp, d)
    xs = x_sample.reshape(bs * ts, d)
    w_in_b = w_in.astype(BF16)
    w_out_b = w_out.astype(BF16)
    zero_state = jnp.zeros((bp, H_RWKV, RWKV_HEAD, RWKV_HEAD), state_wkv.dtype)
    zero_shift = jnp.zeros((bp, 1, D_SHIFT), state_shift.dtype)
    row = lambda a: a.reshape(1, -1)

    outs = {name: [] for name in ("sp", "hp", "ss", "hs")}
    kv_p = [jnp.zeros((depth, bp * tp, C_DIFF), F32) for _ in range(2)]
    kv_s = [jnp.zeros((depth, bs * ts, C_DIFF), F32) for _ in range(2)]
    for l in range(depth):
        lam_init = _lambda_init(l)
        g = row(norm_g[l])
        w_sh = w_in_b[l, :, :D_SHIFT]
        w_rest = w_in_b[l, :, D_SHIFT:]
        rw = (row(shift_mu[l]), row(w_decay0[l]), w_decay2[l], row(a0[l]), w_a2[l], row(k_k[l]), row(k_a[l]),
              row(r_k[l]), row(lnx_w[l]), row(lnx_b[l]))
        lam = (row(lam_q1[l]), row(lam_k1[l]), row(lam_q2[l]), row(lam_k2[l]), row(subln_g[l]))

        def stream(x, bsz, kv, s0, prev0, attn):
            sh = _norm_matmul(x, g, w_sh, 1024, 640)
            gq, kv[0], kv[1], k16, v16 = _proj_rest(x, g, w_rest, kv[0], kv[1], l, 512)
            y_r, s_t = _rwkv_mix(sh, gq, s0, prev0, *rw)
            y_d = attn(gq, k16, v16)
            x = _out_matmul(x, y_r, y_d, w_out_b[l], 512, 1024)
            shift = sh.reshape(bsz, sh.shape[0] // bsz, D_SHIFT)[:, -1]
            return x, s_t, shift

        xp, sn, hn = stream(
            xp, bp, kv_p, zero_state, zero_shift,
            lambda gq, k16, v16: _attn_prompt(gq, k16, v16, *lam, lam_init, 512))
        outs["sp"].append(sn); outs["hp"].append(hn)
        xs, sn, hn = stream(
            xs, bs, kv_s, state_wkv[l], state_shift[l].reshape(bs, 1, D_SHIFT),
            lambda gq, k16, v16: _attn_sample(gq, k16, v16, cache_k, cache_v, l, *lam, lam_init, 1024))
        outs["ss"].append(sn); outs["hs"].append(hn)

    yp = _final_norm(xp, row(final_g), 512).reshape(bp, tp, d)
    ys = _final_norm(xs, row(final_g), 512).reshape(bs, ts, d)
    st = lambda name: jnp.stack(outs[name])
    heads = lambda a, bsz, t: a.reshape(depth, bsz, t, H_DIFF, DV_DIFF)
    return (yp, ys, heads(kv_p[0], bp, tp), heads(kv_p[1], bp, tp), st("sp"), st("hp"),
            heads(kv_s[0], bs, ts), heads(kv_s[1], bs, ts), st("ss"), st("hs"))
```

```python
import functools
import math

import jax
import jax.numpy as jnp
from jax import lax
from jax.experimental import pallas as pl
from jax.experimental.pallas import tpu as pltpu

F32 = jnp.float32
BF16 = jnp.bfloat16

D_MODEL = 2048
CHUNK = 64
C_RWKV = 1024
RWKV_HEAD = 64
H_RWKV = C_RWKV // RWKV_HEAD
N_PAIR = H_RWKV // 2
D_LORA = 64
D_SHIFT = 3 * C_RWKV + 2 * D_LORA
C_DIFF = 1024
H_DIFF = 8
DV_DIFF = 128
DK_DIFF = 64
D_REST = C_RWKV + 4 * C_DIFF
NORM_EPS = 1e-6
RWKV_LN_EPS = 64e-5
SUBLN_EPS = 1e-5
NEG_INF = -1e30
LANES = 128
L_CHUNK = 64
EXP_M05 = math.exp(-0.5)
LOG2E = math.log2(math.e)
ROW_GROUP = 16
HEADS_PER_STEP = 4
VMEM_LIMIT = 48 * 1024 * 1024


def _lambda_init(l):
    return 0.8 - 0.6 * math.exp(-0.3 * l)


def _cparams(sem, vmem_limit=VMEM_LIMIT):
    return pltpu.CompilerParams(dimension_semantics=sem, vmem_limit_bytes=vmem_limit)


_NN = (((1,), (0,)), ((), ()))
_NT = (((1,), (1,)), ((), ()))
_TN = (((0,), (0,)), ((), ()))
_BNN = (((2,), (1,)), ((0,), (0,)))
_BNT = (((2,), (2,)), ((0,), (0,)))
_BTN = (((1,), (1,)), ((0,), (0,)))


def _dot(a, b, dims=_NN):
    return lax.dot_general(a, b, dims, preferred_element_type=F32)


def _split2(x):
    hi = x.astype(BF16)
    lo = (x - hi.astype(F32)).astype(BF16)
    return hi, lo


def _split3(x):
    hi = x.astype(BF16)
    r1 = x - hi.astype(F32)
    mid = r1.astype(BF16)
    lo = (r1 - mid.astype(F32)).astype(BF16)
    return hi, mid, lo


def _mm3(a, b, dims=_NN):
    ah, al = _split2(a)
    bh, bl = _split2(b)
    return _dot(ah, bh, dims) + (_dot(ah, bl, dims) + _dot(al, bh, dims))


def _mm_exact_lhs(a_bf16, b, dims=_NN):
    bh, bm, bl = _split3(b)
    return _dot(a_bf16, bh, dims) + (_dot(a_bf16, bm, dims) + _dot(a_bf16, bl, dims))


def _sum_heads(a, e2):
    return _dot(a.astype(BF16), e2)


def _sigmoid(x):
    return 1.0 / (1.0 + jnp.exp(-x))


def _norm_matmul_kernel(x_ref, g_ref, w_ref, o_ref, h_ref):
    @pl.when(pl.program_id(1) == 0)
    def _():
        x = x_ref[...]
        ms = jnp.mean(x * x, axis=-1, keepdims=True)
        h_ref[...] = (x * lax.rsqrt(ms + NORM_EPS) * g_ref[...]).astype(BF16)

    o_ref[...] = jnp.dot(h_ref[...], w_ref[...], preferred_element_type=F32)


def _norm_matmul(x, g, w, tm, tn):
    m, d = x.shape
    n = w.shape[1]
    return pl.pallas_call(
        _norm_matmul_kernel,
        grid=(m // tm, n // tn),
        in_specs=[
            pl.BlockSpec((tm, d), lambda i, j: (i, 0)),
            pl.BlockSpec((1, d), lambda i, j: (0, 0)),
            pl.BlockSpec((d, tn), lambda i, j: (0, j)),
        ],
        out_specs=pl.BlockSpec((tm, tn), lambda i, j: (i, j)),
        out_shape=jax.ShapeDtypeStruct((m, n), F32),
        scratch_shapes=[pltpu.VMEM((tm, d), BF16)],
        compiler_params=_cparams(("parallel", "arbitrary")),
        name="norm_matmul",
    )(x, g, w)


def _proj_rest_kernel(x_ref, g_ref, w_ref, kin_ref, vin_ref, o_ref, k_ref, v_ref, k16_ref, v16_ref, h_ref):
    del kin_ref, vin_ref
    j = pl.program_id(1)

    @pl.when(j == 0)
    def _():
        x = x_ref[...]
        ms = jnp.mean(x * x, axis=-1, keepdims=True)
        h_ref[...] = (x * lax.rsqrt(ms + NORM_EPS) * g_ref[...]).astype(BF16)

    acc = jnp.dot(h_ref[...], w_ref[...], preferred_element_type=F32)

    @pl.when((j != _KCOL) & (j != _VCOL))
    def _():
        o_ref[...] = acc

    def by_head(ref16):
        for h in range(H_DIFF):
            ref16[h] = acc[:, h * DV_DIFF:(h + 1) * DV_DIFF].astype(BF16)

    @pl.when(j == _KCOL)
    def _():
        k_ref[...] = acc
        by_head(k16_ref)

    @pl.when(j == _VCOL)
    def _():
        v_ref[...] = acc
        by_head(v16_ref)


_KCOL, _VCOL = 2, 3


def _proj_rest(x, g, w, k_all, v_all, layer, tm):
    m, d = x.shape
    tn = C_DIFF
    assert w.shape[1] == 5 * tn and C_RWKV == tn
    ocol = lambda j: jnp.minimum(j, 1) + j // 4
    return pl.pallas_call(
        _proj_rest_kernel,
        grid=(m // tm, 5),
        in_specs=[
            pl.BlockSpec((tm, d), lambda i, j: (i, 0)),
            pl.BlockSpec((1, d), lambda i, j: (0, 0)),
            pl.BlockSpec((d, tn), lambda i, j: (0, j)),
            pl.BlockSpec(memory_space=pl.ANY),
            pl.BlockSpec(memory_space=pl.ANY),
        ],
        out_specs=[
            pl.BlockSpec((tm, tn), lambda i, j: (i, ocol(j))),
            pl.BlockSpec((None, tm, tn), lambda i, j: (layer, i, 0)),
            pl.BlockSpec((None, tm, tn), lambda i, j: (layer, i, 0)),
            pl.BlockSpec((H_DIFF, tm, DV_DIFF), lambda i, j: (0, i, 0)),
            pl.BlockSpec((H_DIFF, tm, DV_DIFF), lambda i, j: (0, i, 0)),
        ],
        out_shape=[
            jax.ShapeDtypeStruct((m, 3 * tn), F32),
            jax.ShapeDtypeStruct(k_all.shape, F32),
            jax.ShapeDtypeStruct(v_all.shape, F32),
            jax.ShapeDtypeStruct((H_DIFF, m, DV_DIFF), BF16),
            jax.ShapeDtypeStruct((H_DIFF, m, DV_DIFF), BF16),
        ],
        input_output_aliases={3: 1, 4: 2},
        scratch_shapes=[pltpu.VMEM((tm, d), BF16)],
        compiler_params=_cparams(("parallel", "arbitrary"),
                                 8 * tm * d + 4 * d * tn + 32 * tm * tn + 2 * tm * d + (4 << 20)),
        name="proj_rest",
    )(x, g, w, k_all, v_all)


def _final_norm_kernel(x_ref, g_ref, o_ref):
    x = x_ref[...]
    ms = jnp.mean(x * x, axis=-1, keepdims=True)
    o_ref[...] = x * lax.rsqrt(ms + NORM_EPS) * g_ref[...]


def _final_norm(x, g, tm):
    m, d = x.shape
    return pl.pallas_call(
        _final_norm_kernel,
        grid=(m // tm,),
        in_specs=[pl.BlockSpec((tm, d), lambda i: (i, 0)), pl.BlockSpec((1, d), lambda i: (0, 0))],
        out_specs=pl.BlockSpec((tm, d), lambda i: (i, 0)),
        out_shape=jax.ShapeDtypeStruct((m, d), F32),
        compiler_params=_cparams(("parallel",)),
        name="final_norm",
    )(x, g)


def _out_matmul_kernel(x_ref, yr_ref, yd_ref, wr_ref, wd_ref, o_ref):
    acc = jnp.dot(yr_ref[...], wr_ref[...], preferred_element_type=F32)
    acc = acc + jnp.dot(yd_ref[...], wd_ref[...], preferred_element_type=F32)
    o_ref[...] = x_ref[...] + acc


def _out_matmul(x, y_r, y_d, w_out, tm, tn):
    m, d = x.shape
    return pl.pallas_call(
        _out_matmul_kernel,
        grid=(m // tm, d // tn),
        in_specs=[
            pl.BlockSpec((tm, tn), lambda i, j: (i, j)),
            pl.BlockSpec((tm, C_RWKV), lambda i, j: (i, 0)),
            pl.BlockSpec((tm, C_DIFF), lambda i, j: (i, 0)),
            pl.BlockSpec((C_RWKV, tn), lambda i, j: (0, j)),
            pl.BlockSpec((C_DIFF, tn), lambda i, j: (1, j)),
        ],
        out_specs=pl.BlockSpec((tm, tn), lambda i, j: (i, j)),
        out_shape=jax.ShapeDtypeStruct((m, d), F32),
        compiler_params=_cparams(("parallel", "parallel")),
        name="out_matmul",
    )(x, y_r, y_d, w_out, w_out)


def _rwkv_kernel(sh_ref, gr_ref, s0_ref, p0_ref, mu_ref, wd0_ref, wd2_ref, a0_ref, wa2_ref,
                 kk_ref, ka_ref, rk_ref, lnw_ref, lnb_ref,
                 y_ref, st_ref,
                 bd_ref, prev_ref):
    c = pl.program_id(1)
    nc = pl.num_programs(1)
    L = L_CHUNK
    C = C_RWKV

    @pl.when(c == 0)
    def _():
        prev_ref[...] = p0_ref[0]
        for p in range(N_PAIR):
            bd_ref[p] = jnp.zeros((LANES, LANES), F32)
            bd_ref[p, 0:RWKV_HEAD, 0:RWKV_HEAD] = s0_ref[0, 2 * p]
            bd_ref[p, RWKV_HEAD:LANES, RWKV_HEAD:LANES] = s0_ref[0, 2 * p + 1]

    sh = sh_ref[...]
    row0 = lax.broadcasted_iota(jnp.int32, sh.shape, 0) == 0
    prev = jnp.where(row0, prev_ref[...], pltpu.roll(sh, 1, 0))
    xs = sh + (prev - sh) * mu_ref[...]
    prev_ref[...] = sh_ref[L - 1:L, :]

    r = xs[:, 0:C]
    kr = xs[:, C:2 * C]
    v = xs[:, 2 * C:3 * C]
    la = xs[:, 3 * C:3 * C + LANES]

    lane = lax.broadcasted_iota(jnp.int32, (L, LANES), 1)
    head0 = lane < RWKV_HEAD
    w2 = jnp.concatenate([wd2_ref[...], wa2_ref[...]], axis=0)
    dw = _mm3(jnp.where(head0, jnp.tanh(la), 0.0), w2)
    da = _mm3(jnp.where(head0, 0.0, la), w2)
    logw = -EXP_M05 * _sigmoid(wd0_ref[...] + dw)
    a = _sigmoid(a0_ref[...] + da)

    er = lax.broadcasted_iota(jnp.int32, (LANES, LANES), 0)
    ec = lax.broadcasted_iota(jnp.int32, (LANES, LANES), 1)
    e2 = jnp.where((er < RWKV_HEAD) == (ec < RWKV_HEAD), 1.0, 0.0).astype(BF16)

    def head_sums(x):
        return jnp.concatenate(
            [_sum_heads(x[:, p * LANES:(p + 1) * LANES], e2) for p in range(N_PAIR)], axis=1)

    kk = kr * kk_ref[...]
    kk = kk / jnp.maximum(jnp.sqrt(head_sums(kk * kk)), 1e-12)
    kw = kr * (1.0 + (a - 1.0) * ka_ref[...])
    b = kk * a

    tr = lax.broadcasted_iota(jnp.int32, (L, L), 0)
    tc = lax.broadcasted_iota(jnp.int32, (L, L), 1)
    tril = jnp.where(tr >= tc, 1.0, 0.0).astype(BF16)
    cum = _mm_exact_lhs(tril, logw)
    cum_l = cum[L - 1:L, :]
    gi = jnp.exp(-cum)
    gl = jnp.exp(cum_l - cum)
    strict = er > ec
    incl = er >= ec
    eye = jnp.where(er == ec, 1.0, 0.0)

    def blockdiag(x):
        tiles = []
        for p in range(N_PAIR):
            xp = x[:, p * LANES:(p + 1) * LANES]
            tiles.append(jnp.concatenate([jnp.where(head0, xp, 0.0), jnp.where(head0, 0.0, xp)], axis=0))
        return jnp.stack(tiles).astype(BF16)

    kt = blockdiag(kk * jnp.exp(cum - logw))
    rt = blockdiag(r * jnp.exp(cum))
    ki = blockdiag(kw * gi)
    bi = blockdiag(b * gi)
    kh = blockdiag(kw * gl)
    bh = blockdiag(b * gl)
    vb = blockdiag(v)
    lhs = jnp.concatenate([kt, rt], axis=1)
    gram = _dot(lhs, jnp.concatenate([ki, bi], axis=1), _BNT)
    a_kk = jnp.where(strict, gram[:, :2 * L, :2 * L], 0.0)
    a_kb = jnp.where(strict, gram[:, :2 * L, 2 * L:], 0.0)
    a_rk = jnp.where(incl, gram[:, 2 * L:, :2 * L], 0.0)
    a_rb = jnp.where(incl, gram[:, 2 * L:, 2 * L:], 0.0)
    t = eye - a_kb
    qb = (-a_kb).astype(BF16)
    qb = _dot(qb, qb, _BNN).astype(BF16)
    for _ in range(4):
        both = _dot(qb, jnp.concatenate([qb, t.astype(BF16)], axis=2), _BNN)
        qb = both[:, :, :2 * L].astype(BF16)
        t = t + both[:, :, 2 * L:]
    t = t + _dot(qb, t.astype(BF16), _BNN)
    s0 = bd_ref[...]
    zy = _dot(lhs, s0.astype(BF16), _BNT)
    z = zy[:, :2 * L] + _dot(a_kk.astype(BF16), vb, _BNN)
    u = _dot(t.astype(BF16), z.astype(BF16), _BNN)
    vu = jnp.concatenate([vb, u.astype(BF16)], axis=1)
    yb = zy[:, 2 * L:] + _dot(jnp.concatenate([a_rk, -a_rb], axis=2).astype(BF16), vu, _BNN)
    g_l = jnp.exp(cum_l)
    decay = jnp.stack([g_l[:, p * LANES:(p + 1) * LANES] for p in range(N_PAIR)])
    bd_ref[...] = s0 * decay + _dot(vu, jnp.concatenate([kh, -bh], axis=1), _BTN)
    y = jnp.concatenate([yb[p, :L] + yb[p, L:] for p in range(N_PAIR)], axis=1)

    d = y - head_sums(y) * (1.0 / RWKV_HEAD)
    var = head_sums(d * d) * (1.0 / RWKV_HEAD)
    yn = d * lax.rsqrt(var + RWKV_LN_EPS) * lnw_ref[...] + lnb_ref[...]
    bonus = head_sums(r * kw * rk_ref[...]) * v
    g = gr_ref[...]
    y_ref[...] = ((yn + bonus) * (g * _sigmoid(g))).astype(y_ref.dtype)

    @pl.when(c == nc - 1)
    def _():
        for p in range(N_PAIR):
            st_ref[0, 2 * p] = bd_ref[p, 0:RWKV_HEAD, 0:RWKV_HEAD]
            st_ref[0, 2 * p + 1] = bd_ref[p, RWKV_HEAD:LANES, RWKV_HEAD:LANES]


def _rwkv_mix(sh, rest, s0, prev0, mu, wd0, wd2, a0, wa2, k_k, k_a, r_k, lnw, lnb):
    bsz = s0.shape[0]
    t = sh.shape[0] // bsz
    assert t % L_CHUNK == 0
    nc = t // L_CHUNK
    L = L_CHUNK
    row = lambda b, c: (b * nc + c, 0)
    vec = lambda n: pl.BlockSpec((1, n), lambda b, c: (0, 0))
    full = lambda shape: pltpu.VMEM(shape, F32)
    return pl.pallas_call(
        _rwkv_kernel,
        grid=(bsz, nc),
        in_specs=[
            pl.BlockSpec((L, D_SHIFT), row),
            pl.BlockSpec((L, C_RWKV), row),
            pl.BlockSpec((1, H_RWKV, RWKV_HEAD, RWKV_HEAD), lambda b, c: (b, 0, 0, 0)),
            pl.BlockSpec((1, 1, D_SHIFT), lambda b, c: (b, 0, 0)),
            vec(D_SHIFT), vec(C_RWKV),
            pl.BlockSpec((D_LORA, C_RWKV), lambda b, c: (0, 0)),
            vec(C_RWKV),
            pl.BlockSpec((D_LORA, C_RWKV), lambda b, c: (0, 0)),
            vec(C_RWKV), vec(C_RWKV), vec(C_RWKV), vec(C_RWKV), vec(C_RWKV),
        ],
        out_specs=[
            pl.BlockSpec((L, C_RWKV), row),
            pl.BlockSpec((1, H_RWKV, RWKV_HEAD, RWKV_HEAD), lambda b, c: (b, 0, 0, 0)),
        ],
        out_shape=[
            jax.ShapeDtypeStruct((bsz * t, C_RWKV), BF16),
            jax.ShapeDtypeStruct((bsz, H_RWKV, RWKV_HEAD, RWKV_HEAD), F32),
        ],
        scratch_shapes=[
            full((N_PAIR, LANES, LANES)), full((1, D_SHIFT)),
        ],
        compiler_params=_cparams(("parallel", "arbitrary")),
        name="rwkv_mix",
    )(sh, rest, s0, prev0, mu, wd0, wd2, a0, wa2, k_k, k_a, r_k, lnw, lnb)


def _stack_maps(q):
    lane = lax.broadcasted_iota(jnp.int32, q.shape, 1)
    m0 = lane < DK_DIFF
    return jnp.concatenate([jnp.where(m0, q, 0.0), jnp.where(m0, 0.0, q)], axis=0).astype(BF16)


def _fold_lanes(x, op):
    out = x[:, 0:LANES]
    for c in range(1, x.shape[1] // LANES):
        out = op(out, x[:, c * LANES:(c + 1) * LANES])
    return out


def _softmax_tile(s_ref, p_ref, mx_ref, alpha_ref, m_ref, bias_rows, l_ref=None):
    nrows, tk = s_ref.shape
    for r0 in range(0, nrows, ROW_GROUP):
        rows = pl.ds(r0, ROW_GROUP)
        mx_ref[rows, :] = _fold_lanes(s_ref[rows, :] + bias_rows(r0), jnp.maximum)
    m_prev = m_ref[...]
    m_new = jnp.maximum(m_prev, jnp.max(mx_ref[...], axis=-1, keepdims=True))
    alpha_ref[...] = jnp.exp2(m_prev - m_new)
    m_ref[...] = m_new
    for r0 in range(0, nrows, ROW_GROUP):
        rows = pl.ds(r0, ROW_GROUP)
        s = s_ref[rows, :] + bias_rows(r0)
        p = jnp.exp2(s - jnp.concatenate([m_ref[rows, :]] * (tk // LANES), axis=1))
        if l_ref is not None:
            l_ref[rows, :] = alpha_ref[rows, :] * l_ref[rows, :] + _fold_lanes(p, jnp.add)
        p_ref[rows, :] = p.astype(BF16)


def _values_and_ones(v):
    return jnp.concatenate([v.astype(BF16), jnp.ones(v.shape, BF16)], axis=1)


def _rescale(alpha):
    return jnp.concatenate([alpha, alpha], axis=1)


def _softmax_reset(m_ref, acc_ref):
    m_ref[...] = jnp.full(m_ref.shape, NEG_INF, F32)
    acc_ref[...] = jnp.zeros(acc_ref.shape, F32)


def _diff_finish(vals, sums, lam, lam_init, subln_g, gate):
    t = vals.shape[0] // 2
    o = vals / sums
    o = o[:t] - lam * o[t:]
    o = o * lax.rsqrt(jnp.mean(o * o, axis=-1, keepdims=True) + SUBLN_EPS) * subln_g * (1.0 - lam_init)
    return o * (gate * _sigmoid(gate))


def _lambda(lq1_ref, lk1_ref, lq2_ref, lk2_ref, lam_init):
    s1 = jnp.sum(lq1_ref[...] * lk1_ref[...], axis=-1, keepdims=True)
    s2 = jnp.sum(lq2_ref[...] * lk2_ref[...], axis=-1, keepdims=True)
    return jnp.exp(s1) - jnp.exp(s2) + lam_init


def _local_bias(slope, nrows, nk, tq):
    i = lax.broadcasted_iota(jnp.int32, (nrows, nk), 0)
    i = jnp.where(i >= tq, i - tq, i)
    j = lax.broadcasted_iota(jnp.int32, (nrows, nk), 1)
    bias = slope * (i - jnp.abs(i - j)).astype(F32)
    shift = CHUNK.bit_length() - 1
    visible = jnp.right_shift(j, shift) <= jnp.right_shift(i, shift)
    return jnp.where(visible, bias, NEG_INF)


def _head_slope(h):
    return lax.bitcast_convert_type(jnp.full((1, 1), (126 - h) << 23, jnp.int32), F32) * LOG2E


def _attn_prompt_kernel(qi_ref, kj_ref, q_ref, k_ref, v_ref, gd_ref, lq1_ref, lk1_ref, lq2_ref, lk2_ref, sg_ref,
                        o_ref, q2_ref, s_ref, p_ref, bias_ref, mx_ref, alpha_ref, m_ref, l_ref, acc_ref,
                        *, tq, tk, lam_init):
    qi = qi_ref[pl.program_id(1)]
    kj = kj_ref[pl.program_id(1)]
    heads = [(n, slice(n * DV_DIFF, (n + 1) * DV_DIFF), _head_slope(pl.program_id(0) * HEADS_PER_STEP + n))
             for n in range(HEADS_PER_STEP)]

    @pl.when((qi == 0) & (kj == 0))
    def _():
        for n, _, slope in heads:
            bias_ref[n] = _local_bias(slope, tq, tk, tq)

    @pl.when(kj == 0)
    def _():
        for n, cols, _ in heads:
            q2_ref[n] = _stack_maps(q_ref[:, cols] * (DK_DIFF ** -0.5 * LOG2E))
        _softmax_reset(m_ref, acc_ref)
        l_ref[...] = jnp.zeros(l_ref.shape, F32)

    def update(n, bias_rows):
        s_ref[n] = _dot(q2_ref[n], k_ref[n], _NT)
        _softmax_tile(s_ref.at[n], p_ref.at[n], mx_ref.at[n], alpha_ref.at[n], m_ref.at[n], bias_rows, l_ref.at[n])
        acc_ref[n] = alpha_ref[n] * acc_ref[n] + _dot(p_ref[n], v_ref[n])

    @pl.when(kj < qi)
    def _():
        j = lax.broadcasted_iota(jnp.int32, (1, tk), 1)
        dist = (j - (qi - kj) * tq).astype(F32)
        for n, _, slope in heads:
            bias = slope * dist
            update(n, lambda r0: bias)

    @pl.when(kj == qi)
    def _():
        lam = _lambda(lq1_ref, lk1_ref, lq2_ref, lk2_ref, lam_init)
        for n, cols, _ in heads:
            update(n, lambda r0: bias_ref[n, pl.ds(r0 % tq, ROW_GROUP), :])
            l = jnp.sum(l_ref[n], axis=-1, keepdims=True)
            o_ref[:, cols] = _diff_finish(acc_ref[n], l, lam, lam_init, sg_ref[...],
                                          gd_ref[:, cols]).astype(o_ref.dtype)


def _attn_prompt(gq, k16, v16, lq1, lk1, lq2, lk2, subln_g, lam_init, tq):
    t = gq.shape[0]
    assert t % tq == 0 and tq % CHUNK == 0
    nq = t // tq
    hs = HEADS_PER_STEP
    wide = hs * DV_DIFF
    assert H_DIFF % hs == 0
    qcol, gcol = C_RWKV // wide, (C_RWKV + C_DIFF) // wide
    past = lambda: pl.BlockSpec((hs, tq, DV_DIFF), lambda h, s, qi, kj: (h, kj[s], 0))
    vec = lambda n: pl.BlockSpec((1, n), lambda h, s, qi, kj: (0, 0))
    per_head = lambda shape, dtype: pltpu.VMEM((hs,) + shape, dtype)
    kern = functools.partial(_attn_prompt_kernel, tq=tq, tk=tq, lam_init=lam_init)
    qi_tab = jnp.asarray([i for i in range(nq) for _ in range(i + 1)], jnp.int32)
    kj_tab = jnp.asarray([j for i in range(nq) for j in range(i + 1)], jnp.int32)
    grid_spec = pltpu.PrefetchScalarGridSpec(
        num_scalar_prefetch=2,
        grid=(H_DIFF // hs, nq * (nq + 1) // 2),
        in_specs=[
            pl.BlockSpec((tq, wide), lambda h, s, qi, kj: (qi[s], qcol + h)),
            past(), past(),
            pl.BlockSpec((tq, wide), lambda h, s, qi, kj: (qi[s], gcol + h)),
            vec(DK_DIFF), vec(DK_DIFF), vec(DK_DIFF), vec(DK_DIFF), vec(DV_DIFF),
        ],
        out_specs=pl.BlockSpec((tq, wide), lambda h, s, qi, kj: (qi[s], h)),
        scratch_shapes=[
            per_head((2 * tq, DV_DIFF), BF16),
            per_head((2 * tq, tq), F32),
            per_head((2 * tq, tq), BF16),
            per_head((tq, tq), F32),
            per_head((2 * tq, LANES), F32), per_head((2 * tq, LANES), F32),
            per_head((2 * tq, LANES), F32),
            per_head((2 * tq, LANES), F32),
            per_head((2 * tq, DV_DIFF), F32),
        ],
    )
    return pl.pallas_call(
        kern,
        grid_spec=grid_spec,
        out_shape=jax.ShapeDtypeStruct((t, C_DIFF), BF16),
        compiler_params=_cparams(("arbitrary", "arbitrary")),
        name="attn_prompt",
    )(qi_tab, kj_tab, gq, k16, v16, gq, lq1, lk1, lq2, lk2, subln_g)


def _attn_sample_kernel(q_ref, kn_ref, vn_ref, gd_ref, ck_ref, cv_ref, lq1_ref, lk1_ref, lq2_ref, lk2_ref, sg_ref,
                        o_ref, q2_ref, s_ref, p_ref, mx_ref, alpha_ref, m_ref, acc_ref,
                        *, t, tk, past_len, lam_init):
    kj = pl.program_id(1)
    nk = pl.num_programs(1) - 1
    heads = [(h, slice(h * DV_DIFF, (h + 1) * DV_DIFF), 2.0 ** -(h + 1) * LOG2E) for h in range(H_DIFF)]

    @pl.when(kj == 0)
    def _():
        for h, cols, _ in heads:
            q2_ref[h] = _stack_maps(q_ref[:, cols] * (DK_DIFF ** -0.5 * LOG2E))
        _softmax_reset(m_ref, acc_ref)

    @pl.when(kj < nk)
    def _():
        j = lax.broadcasted_iota(jnp.int32, (1, tk), 1)
        dist = (j + kj * tk - past_len).astype(F32)
        for h, cols, slope in heads:
            head_rows = pl.ds(h, tk, stride=H_DIFF)
            bias = slope * dist
            s_ref[h] = _dot(q2_ref[h], ck_ref[head_rows, :].astype(BF16), _NT)
            _softmax_tile(s_ref.at[h], p_ref.at[h], mx_ref.at[h], alpha_ref.at[h], m_ref.at[h], lambda r0: bias)
            acc_ref[h] = (_rescale(alpha_ref[h]) * acc_ref[h]
                          + _dot(p_ref[h], _values_and_ones(cv_ref[head_rows, :])))

    @pl.when(kj == nk)
    def _():
        lam = _lambda(lq1_ref, lk1_ref, lq2_ref, lk2_ref, lam_init)
        for h, cols, slope in heads:
            s = _dot(q2_ref[h], kn_ref[h], _NT) + _local_bias(slope, 2 * t, t, t)
            m_prev = m_ref[h]
            m_new = jnp.maximum(m_prev, jnp.max(s, axis=-1, keepdims=True))
            alpha = jnp.exp2(m_prev - m_new)
            p = jnp.exp2(s - m_new[:, 0:t]).astype(BF16)
            acc = _rescale(alpha) * acc_ref[h] + _dot(p, _values_and_ones(vn_ref[h]))
            o_ref[:, cols] = _diff_finish(acc[:, :DV_DIFF], acc[:, DV_DIFF:], lam, lam_init, sg_ref[...],
                                          gd_ref[:, cols]).astype(o_ref.dtype)


def _attn_sample(gq, k16, v16, cache_k, cache_v, layer, lq1, lk1, lq2, lk2, subln_g, lam_init, tk):
    depth, bsz, past_len = cache_k.shape[:3]
    cache_k = cache_k.reshape(depth, bsz, past_len * H_DIFF, DV_DIFF)
    cache_v = cache_v.reshape(depth, bsz, past_len * H_DIFF, DV_DIFF)
    t = gq.shape[0] // bsz
    assert t <= CHUNK and past_len % CHUNK == 0 and past_len % tk == 0
    nk = past_len // tk
    vec = lambda n: pl.BlockSpec((1, n), lambda b, j: (0, 0))
    new = lambda col: pl.BlockSpec((t, C_DIFF), lambda b, j: (b, col))
    fresh = pl.BlockSpec((H_DIFF, t, DV_DIFF), lambda b, j: (0, b, 0))
    past = pl.BlockSpec((None, None, tk * H_DIFF, DV_DIFF),
                        lambda b, j: (layer, b, jnp.minimum(j, nk - 1), 0))
    kern = functools.partial(_attn_sample_kernel, t=t, tk=tk, past_len=past_len, lam_init=lam_init)
    per_head = lambda shape, dtype: pltpu.VMEM((H_DIFF,) + shape, dtype)
    return pl.pallas_call(
        kern,
        grid=(bsz, nk + 1),
        in_specs=[new(1), fresh, fresh, new(2), past, past,
                  vec(DK_DIFF), vec(DK_DIFF), vec(DK_DIFF), vec(DK_DIFF), vec(DV_DIFF)],
        out_specs=pl.BlockSpec((t, C_DIFF), lambda b, j: (b, 0)),
        out_shape=jax.ShapeDtypeStruct((bsz * t, C_DIFF), BF16),
        scratch_shapes=[
            per_head((2 * t, DV_DIFF), BF16),
            per_head((2 * t, tk), F32),
            per_head((2 * t, tk), BF16),
            per_head((2 * t, LANES), F32), per_head((2 * t, LANES), F32),
            per_head((2 * t, LANES), F32),
            per_head((2 * t, 2 * DV_DIFF), F32),
        ],
        compiler_params=_cparams(("parallel", "arbitrary")),
        name="attn_sample",
    )(gq, k16, v16, gq, cache_k, cache_v, lq1, lk1, lq2, lk2, subln_g)


def kernel(x_prompt, x_sample, cache_k, cache_v, state_wkv, state_shift, norm_g, w_in, shift_mu, w_decay0, w_decay2,
           a0, w_a2, k_k, k_a, r_k, lnx_w, lnx_b, lam_q1, lam_k1, lam_q2, lam_k2, subln_g, w_out, final_g):
    depth = w_in.shape[0]
    bp, tp, d = x_prompt.shape
    bs, ts, _ = x_sample.shape
    assert bp == 1, "the prompt attention kernel handles one sequence"

    xp = x_prompt.reshape(bp * tp, d)
    xs = x_sample.reshape(bs * ts, d)
    w_in_b = w_in.astype(BF16)
    w_out_b = w_out.astype(BF16)
    zero_state = jnp.zeros((bp, H_RWKV, RWKV_HEAD, RWKV_HEAD), state_wkv.dtype)
    zero_shift = jnp.zeros((bp, 1, D_SHIFT), state_shift.dtype)
    row = lambda a: a.reshape(1, -1)

    outs = {name: [] for name in ("sp", "hp", "ss", "hs")}
    kv_p = [jnp.zeros((depth, bp * tp, C_DIFF), F32) for _ in range(2)]
    kv_s = [jnp.zeros((depth, bs * ts, C_DIFF), F32) for _ in range(2)]
    for l in range(depth):
        lam_init = _lambda_init(l)
        g = row(norm_g[l])
        w_sh = w_in_b[l, :, :D_SHIFT]
        w_rest = w_in_b[l, :, D_SHIFT:]
        rw = (row(shift_mu[l]), row(w_decay0[l]), w_decay2[l], row(a0[l]), w_a2[l], row(k_k[l]), row(k_a[l]),
              row(r_k[l]), row(lnx_w[l]), row(lnx_b[l]))
        lam = (row(lam_q1[l]), row(lam_k1[l]), row(lam_q2[l]), row(lam_k2[l]), row(subln_g[l]))

        def stream(x, bsz, kv, s0, prev0, attn):
            sh = _norm_matmul(x, g, w_sh, 1024, 640)
            gq, kv[0], kv[1], k16, v16 = _proj_rest(x, g, w_rest, kv[0], kv[1], l, 512)
            y_r, s_t = _rwkv_mix(sh, gq, s0, prev0, *rw)
            y_d = attn(gq, k16, v16)
            x = _out_matmul(x, y_r, y_d, w_out_b[l], 512, 1024)
            shift = sh.reshape(bsz, sh.shape[0] // bsz, D_SHIFT)[:, -1]
            return x, s_t, shift

        xp, sn, hn = stream(
            xp, bp, kv_p, zero_state, zero_shift,
            lambda gq, k16, v16: _attn_prompt(gq, k16, v16, *lam, lam_init, 512))
        outs["sp"].append(sn); outs["hp"].append(hn)
        xs, sn, hn = stream(
            xs, bs, kv_s, state_wkv[l], state_shift[l].reshape(bs, 1, D_SHIFT),
            lambda gq, k16, v16: _attn_sample(gq, k16, v16, cache_k, cache_v, l, *lam, lam_init, 1024))
        outs["ss"].append(sn); outs["hs"].append(hn)

    yp = _final_norm(xp, row(final_g), 512).reshape(bp, tp, d)
    ys = _final_norm(xs, row(final_g), 512).reshape(bs, ts, d)
    st = lambda name: jnp.stack(outs[name])
    heads = lambda a, bsz, t: a.reshape(depth, bsz, t, H_DIFF, DV_DIFF)
    return (yp, ys, heads(kv_p[0], bp, tp), heads(kv_p[1], bp, tp), st("sp"), st("hp"),
            heads(kv_s[0], bs, ts), heads(kv_s[1], bs, ts), st("ss"), st("hs"))
```

```python
import functools
import math

import jax
import jax.numpy as jnp
from jax import lax
from jax.experimental import pallas as pl
from jax.experimental.pallas import tpu as pltpu

F32 = jnp.float32
BF16 = jnp.bfloat16

D_MODEL = 2048
CHUNK = 64
C_RWKV = 1024
RWKV_HEAD = 64
H_RWKV = C_RWKV // RWKV_HEAD
N_PAIR = H_RWKV // 2
D_LORA = 64
D_SHIFT = 3 * C_RWKV + 2 * D_LORA
C_DIFF = 1024
H_DIFF = 8
DV_DIFF = 128
DK_DIFF = 64
D_REST = C_RWKV + 4 * C_DIFF
NORM_EPS = 1e-6
RWKV_LN_EPS = 64e-5
SUBLN_EPS = 1e-5
NEG_INF = -1e30
LANES = 128
L_CHUNK = 64
EXP_M05 = math.exp(-0.5)
LOG2E = math.log2(math.e)
ROW_GROUP = 16
HEADS_PER_STEP = 4
VMEM_LIMIT = 48 * 1024 * 1024


def _lambda_init(l):
    return 0.8 - 0.6 * math.exp(-0.3 * l)


def _cparams(sem, vmem_limit=VMEM_LIMIT):
    return pltpu.CompilerParams(dimension_semantics=sem, vmem_limit_bytes=vmem_limit)


_NN = (((1,), (0,)), ((), ()))
_NT = (((1,), (1,)), ((), ()))
_TN = (((0,), (0,)), ((), ()))
_BNN = (((2,), (1,)), ((0,), (0,)))
_BNT = (((2,), (2,)), ((0,), (0,)))
_BTN = (((1,), (1,)), ((0,), (0,)))


def _dot(a, b, dims=_NN):
    return lax.dot_general(a, b, dims, preferred_element_type=F32)


def _split2(x):
    hi = x.astype(BF16)
    lo = (x - hi.astype(F32)).astype(BF16)
    return hi, lo


def _split3(x):
    hi = x.astype(BF16)
    r1 = x - hi.astype(F32)
    mid = r1.astype(BF16)
    lo = (r1 - mid.astype(F32)).astype(BF16)
    return hi, mid, lo


def _mm3(a, b, dims=_NN):
    ah, al = _split2(a)
    bh, bl = _split2(b)
    return _dot(ah, bh, dims) + (_dot(ah, bl, dims) + _dot(al, bh, dims))


def _mm_exact_lhs(a_bf16, b, dims=_NN):
    bh, bm, bl = _split3(b)
    return _dot(a_bf16, bh, dims) + (_dot(a_bf16, bm, dims) + _dot(a_bf16, bl, dims))


def _sum_heads(a, e2):
    return _dot(a.astype(BF16), e2)


def _sigmoid(x):
    return 1.0 / (1.0 + jnp.exp(-x))


def _norm_matmul_kernel(x_ref, g_ref, w_ref, o_ref, h_ref):
    @pl.when(pl.program_id(1) == 0)
    def _():
        x = x_ref[...]
        ms = jnp.mean(x * x, axis=-1, keepdims=True)
        h_ref[...] = (x * lax.rsqrt(ms + NORM_EPS) * g_ref[...]).astype(BF16)

    o_ref[...] = jnp.dot(h_ref[...], w_ref[...], preferred_element_type=F32)


def _norm_matmul(x, g, w, tm, tn):
    m, d = x.shape
    n = w.shape[1]
    return pl.pallas_call(
        _norm_matmul_kernel,
        grid=(m // tm, n // tn),
        in_specs=[
            pl.BlockSpec((tm, d), lambda i, j: (i, 0)),
            pl.BlockSpec((1, d), lambda i, j: (0, 0)),
            pl.BlockSpec((d, tn), lambda i, j: (0, j)),
        ],
        out_specs=pl.BlockSpec((tm, tn), lambda i, j: (i, j)),
        out_shape=jax.ShapeDtypeStruct((m, n), F32),
        scratch_shapes=[pltpu.VMEM((tm, d), BF16)],
        compiler_params=_cparams(("parallel", "arbitrary")),
        name="norm_matmul",
    )(x, g, w)


def _proj_rest_kernel(x_ref, g_ref, w_ref, kin_ref, vin_ref, o_ref, k_ref, v_ref, k16_ref, v16_ref, h_ref):
    del kin_ref, vin_ref
    j = pl.program_id(1)

    @pl.when(j == 0)
    def _():
        x = x_ref[...]
        ms = jnp.mean(x * x, axis=-1, keepdims=True)
        h_ref[...] = (x * lax.rsqrt(ms + NORM_EPS) * g_ref[...]).astype(BF16)

    acc = jnp.dot(h_ref[...], w_ref[...], preferred_element_type=F32)

    @pl.when((j != _KCOL) & (j != _VCOL))
    def _():
        o_ref[...] = acc

    def by_head(ref16):
        for h in range(H_DIFF):
            ref16[h] = acc[:, h * DV_DIFF:(h + 1) * DV_DIFF].astype(BF16)

    @pl.when(j == _KCOL)
    def _():
        k_ref[...] = acc
        by_head(k16_ref)

    @pl.when(j == _VCOL)
    def _():
        v_ref[...] = acc
        by_head(v16_ref)


_KCOL, _VCOL = 2, 3


def _proj_rest(x, g, w, k_all, v_all, layer, tm):
    m, d = x.shape
    tn = C_DIFF
    assert w.shape[1] == 5 * tn and C_RWKV == tn
    ocol = lambda j: jnp.minimum(j, 1) + j // 4
    return pl.pallas_call(
        _proj_rest_kernel,
        grid=(m // tm, 5),
        in_specs=[
            pl.BlockSpec((tm, d), lambda i, j: (i, 0)),
            pl.BlockSpec((1, d), lambda i, j: (0, 0)),
            pl.BlockSpec((d, tn), lambda i, j: (0, j)),
            pl.BlockSpec(memory_space=pl.ANY),
            pl.BlockSpec(memory_space=pl.ANY),
        ],
        out_specs=[
            pl.BlockSpec((tm, tn), lambda i, j: (i, ocol(j))),
            pl.BlockSpec((None, tm, tn), lambda i, j: (layer, i, 0)),
            pl.BlockSpec((None, tm, tn), lambda i, j: (layer, i, 0)),
            pl.BlockSpec((H_DIFF, tm, DV_DIFF), lambda i, j: (0, i, 0)),
            pl.BlockSpec((H_DIFF, tm, DV_DIFF), lambda i, j: (0, i, 0)),
        ],
        out_shape=[
            jax.ShapeDtypeStruct((m, 3 * tn), F32),
            jax.ShapeDtypeStruct(k_all.shape, F32),
            jax.ShapeDtypeStruct(v_all.shape, F32),
            jax.ShapeDtypeStruct((H_DIFF, m, DV_DIFF), BF16),
            jax.ShapeDtypeStruct((H_DIFF, m, DV_DIFF), BF16),
        ],
        input_output_aliases={3: 1, 4: 2},
        scratch_shapes=[pltpu.VMEM((tm, d), BF16)],
        compiler_params=_cparams(("parallel", "arbitrary"),
                                 8 * tm * d + 4 * d * tn + 32 * tm * tn + 2 * tm * d + (4 << 20)),
        name="proj_rest",
    )(x, g, w, k_all, v_all)


def _final_norm_kernel(x_ref, g_ref, o_ref):
    x = x_ref[...]
    ms = jnp.mean(x * x, axis=-1, keepdims=True)
    o_ref[...] = x * lax.rsqrt(ms + NORM_EPS) * g_ref[...]


def _final_norm(x, g, tm):
    m, d = x.shape
    return pl.pallas_call(
        _final_norm_kernel,
        grid=(m // tm,),
        in_specs=[pl.BlockSpec((tm, d), lambda i: (i, 0)), pl.BlockSpec((1, d), lambda i: (0, 0))],
        out_specs=pl.BlockSpec((tm, d), lambda i: (i, 0)),
        out_shape=jax.ShapeDtypeStruct((m, d), F32),
        compiler_params=_cparams(("parallel",)),
        name="final_norm",
    )(x, g)


def _out_matmul_kernel(x_ref, yr_ref, yd_ref, wr_ref, wd_ref, o_ref):
    acc = jnp.dot(yr_ref[...], wr_ref[...], preferred_element_type=F32)
    acc = acc + jnp.dot(yd_ref[...], wd_ref[...], preferred_element_type=F32)
    o_ref[...] = x_ref[...] + acc


def _out_matmul(x, y_r, y_d, w_out, tm, tn):
    m, d = x.shape
    return pl.pallas_call(
        _out_matmul_kernel,
        grid=(m // tm, d // tn),
        in_specs=[
            pl.BlockSpec((tm, tn), lambda i, j: (i, j)),
            pl.BlockSpec((tm, C_RWKV), lambda i, j: (i, 0)),
            pl.BlockSpec((tm, C_DIFF), lambda i, j: (i, 0)),
            pl.BlockSpec((C_RWKV, tn), lambda i, j: (0, j)),
            pl.BlockSpec((C_DIFF, tn), lambda i, j: (1, j)),
        ],
        out_specs=pl.BlockSpec((tm, tn), lambda i, j: (i, j)),
        out_shape=jax.ShapeDtypeStruct((m, d), F32),
        compiler_params=_cparams(("parallel", "parallel")),
        name="out_matmul",
    )(x, y_r, y_d, w_out, w_out)


def _rwkv_kernel(sh_ref, gr_ref, s0_ref, p0_ref, mu_ref, wd0_ref, wd2_ref, a0_ref, wa2_ref,
                 kk_ref, ka_ref, rk_ref, lnw_ref, lnb_ref,
                 y_ref, st_ref,
                 bd_ref, prev_ref, *, nsub):
    c = pl.program_id(1)
    nc = pl.num_programs(1)
    L = L_CHUNK
    R = nsub * L
    C = C_RWKV

    @pl.when(c == 0)
    def _():
        prev_ref[...] = p0_ref[0]
        for p in range(N_PAIR):
            bd_ref[p] = jnp.zeros((LANES, LANES), F32)
            bd_ref[p, 0:RWKV_HEAD, 0:RWKV_HEAD] = s0_ref[0, 2 * p]
            bd_ref[p, RWKV_HEAD:LANES, RWKV_HEAD:LANES] = s0_ref[0, 2 * p + 1]

    sh = sh_ref[...]
    row0 = lax.broadcasted_iota(jnp.int32, sh.shape, 0) == 0
    prev = jnp.where(row0, prev_ref[...], pltpu.roll(sh, 1, 0))
    xs = sh + (prev - sh) * mu_ref[...]
    prev_ref[...] = sh_ref[R - 1:R, :]

    r = xs[:, 0:C]
    kr = xs[:, C:2 * C]
    v = xs[:, 2 * C:3 * C]
    la = xs[:, 3 * C:3 * C + LANES]

    lane = lax.broadcasted_iota(jnp.int32, (L, LANES), 1)
    head0 = lane < RWKV_HEAD
    w2 = jnp.concatenate([wd2_ref[...], wa2_ref[...]], axis=0)
    w_lo = lax.broadcasted_iota(jnp.int32, la.shape, 1) < D_LORA
    dw = _mm3(jnp.where(w_lo, jnp.tanh(la), 0.0), w2)
    da = _mm3(jnp.where(w_lo, 0.0, la), w2)
    logw = -EXP_M05 * _sigmoid(wd0_ref[...] + dw)
    a = _sigmoid(a0_ref[...] + da)

    er = lax.broadcasted_iota(jnp.int32, (LANES, LANES), 0)
    ec = lax.broadcasted_iota(jnp.int32, (LANES, LANES), 1)
    e2 = jnp.where((er < RWKV_HEAD) == (ec < RWKV_HEAD), 1.0, 0.0).astype(BF16)

    def head_sums(x):
        return jnp.concatenate(
            [_sum_heads(x[:, p * LANES:(p + 1) * LANES], e2) for p in range(N_PAIR)], axis=1)

    kk = kr * kk_ref[...]
    kk = kk / jnp.maximum(jnp.sqrt(head_sums(kk * kk)), 1e-12)
    kw = kr * (1.0 + (a - 1.0) * ka_ref[...])
    b = kk * a

    tr = lax.broadcasted_iota(jnp.int32, (R, R), 0)
    tc = lax.broadcasted_iota(jnp.int32, (R, R), 1)
    lbits = L.bit_length() - 1
    tril = jnp.where((tr >= tc) & (jnp.right_shift(tr, lbits) == jnp.right_shift(tc, lbits)), 1.0, 0.0).astype(BF16)
    cum = _mm_exact_lhs(tril, logw)
    cum_l = [cum[n * L + L - 1:(n + 1) * L, :] for n in range(nsub)]
    gi = jnp.exp(-cum)
    gl = jnp.exp(jnp.concatenate([jnp.broadcast_to(cl, (L, C)) for cl in cum_l], axis=0) - cum)
    srow = lax.broadcasted_iota(jnp.int32, (L, LANES), 0)
    scol = jnp.where(head0, lane, lane - RWKV_HEAD)
    strict = srow > scol
    incl = srow >= scol
    eye = jnp.where(srow == scol, 1.0, 0.0)
    same_head = (er < RWKV_HEAD) == (ec < RWKV_HEAD)

    def side(x):
        return jnp.stack([x[n * L:(n + 1) * L, p * LANES:(p + 1) * LANES]
                          for n in range(nsub) for p in range(N_PAIR)])

    def expand(xs):
        return jnp.concatenate([jnp.where(head0, xs, 0.0), jnp.where(head0, 0.0, xs)], axis=1).astype(BF16)

    v_s = side(v).astype(BF16)
    vb = expand(side(v))
    lhs = jnp.concatenate([side(kk * jnp.exp(cum - logw)), side(r * jnp.exp(cum))], axis=1).astype(BF16)
    gram = _dot(lhs, jnp.concatenate([expand(side(kw * gi)), expand(side(b * gi))], axis=1), _BNT)
    a_kk = jnp.where(strict, gram[:, :L, :2 * L], 0.0)
    a_kb = jnp.where(strict, gram[:, :L, 2 * L:], 0.0)
    a_rk = jnp.where(incl, gram[:, L:, :2 * L], 0.0)
    a_rb = jnp.where(incl, gram[:, L:, 2 * L:], 0.0)
    t = eye - a_kb
    q = _dot((-a_kb).astype(BF16), expand(-a_kb), _BNN)
    for _ in range(4):
        both = _dot(q.astype(BF16), jnp.concatenate([expand(q), expand(t)], axis=2), _BNN)
        q = both[:, :, :2 * L]
        t = t + both[:, :, 2 * L:]
    t = t + _dot(q.astype(BF16), expand(t), _BNN)
    t = t.astype(BF16)
    a_kk = a_kk.astype(BF16)
    a_r = jnp.concatenate([a_rk, -a_rb], axis=2).astype(BF16)
    kbh = jnp.concatenate([side(kw * gl), -side(b * gl)], axis=1).astype(BF16)
    state = bd_ref[...]
    y_rows = []
    for n in range(nsub):
        sub = slice(n * N_PAIR, (n + 1) * N_PAIR)
        zy = _dot(lhs[sub], state.astype(BF16), _BNT)
        z = zy[:, :L] + _dot(a_kk[sub], vb[sub], _BNN)
        u = _dot(t[sub], expand(z), _BNN)
        ys = zy[:, L:] + _dot(a_r[sub], jnp.concatenate([vb[sub], expand(u)], axis=1), _BNN)
        g_l = jnp.exp(cum_l[n])
        decay = jnp.stack([g_l[:, p * LANES:(p + 1) * LANES] for p in range(N_PAIR)])
        grow = _dot(jnp.concatenate([v_s[sub], u.astype(BF16)], axis=1), kbh[sub], _BTN)
        state = state * decay + jnp.where(same_head, grow, 0.0)
        y_rows.append(jnp.concatenate([ys[p] for p in range(N_PAIR)], axis=1))
    bd_ref[...] = state
    y = jnp.concatenate(y_rows, axis=0)

    d = y - head_sums(y) * (1.0 / RWKV_HEAD)
    var = head_sums(d * d) * (1.0 / RWKV_HEAD)
    yn = d * lax.rsqrt(var + RWKV_LN_EPS) * lnw_ref[...] + lnb_ref[...]
    bonus = head_sums(r * kw * rk_ref[...]) * v
    g = gr_ref[...]
    y_ref[...] = ((yn + bonus) * (g * _sigmoid(g))).astype(y_ref.dtype)

    @pl.when(c == nc - 1)
    def _():
        for p in range(N_PAIR):
            st_ref[0, 2 * p] = bd_ref[p, 0:RWKV_HEAD, 0:RWKV_HEAD]
            st_ref[0, 2 * p + 1] = bd_ref[p, RWKV_HEAD:LANES, RWKV_HEAD:LANES]


def _rwkv_mix(sh, rest, s0, prev0, mu, wd0, wd2, a0, wa2, k_k, k_a, r_k, lnw, lnb):
    bsz = s0.shape[0]
    t = sh.shape[0] // bsz
    assert t % L_CHUNK == 0
    nsub = max(n for n in (1, 2, 4) if t % (n * L_CHUNK) == 0)
    L = nsub * L_CHUNK
    nc = t // L
    row = lambda b, c: (b * nc + c, 0)
    vec = lambda n: pl.BlockSpec((1, n), lambda b, c: (0, 0))
    full = lambda shape: pltpu.VMEM(shape, F32)
    return pl.pallas_call(
        functools.partial(_rwkv_kernel, nsub=nsub),
        grid=(bsz, nc),
        in_specs=[
            pl.BlockSpec((L, D_SHIFT), row),
            pl.BlockSpec((L, C_RWKV), row),
            pl.BlockSpec((1, H_RWKV, RWKV_HEAD, RWKV_HEAD), lambda b, c: (b, 0, 0, 0)),
            pl.BlockSpec((1, 1, D_SHIFT), lambda b, c: (b, 0, 0)),
            vec(D_SHIFT), vec(C_RWKV),
            pl.BlockSpec((D_LORA, C_RWKV), lambda b, c: (0, 0)),
            vec(C_RWKV),
            pl.BlockSpec((D_LORA, C_RWKV), lambda b, c: (0, 0)),
            vec(C_RWKV), vec(C_RWKV), vec(C_RWKV), vec(C_RWKV), vec(C_RWKV),
        ],
        out_specs=[
            pl.BlockSpec((L, C_RWKV), row),
            pl.BlockSpec((1, H_RWKV, RWKV_HEAD, RWKV_HEAD), lambda b, c: (b, 0, 0, 0)),
        ],
        out_shape=[
            jax.ShapeDtypeStruct((bsz * t, C_RWKV), BF16),
            jax.ShapeDtypeStruct((bsz, H_RWKV, RWKV_HEAD, RWKV_HEAD), F32),
        ],
        scratch_shapes=[
            full((N_PAIR, LANES, LANES)), full((1, D_SHIFT)),
        ],
        compiler_params=_cparams(("parallel", "arbitrary")),
        name="rwkv_mix",
    )(sh, rest, s0, prev0, mu, wd0, wd2, a0, wa2, k_k, k_a, r_k, lnw, lnb)


def _stack_maps(q):
    lane = lax.broadcasted_iota(jnp.int32, q.shape, 1)
    m0 = lane < DK_DIFF
    return jnp.concatenate([jnp.where(m0, q, 0.0), jnp.where(m0, 0.0, q)], axis=0).astype(BF16)


def _fold_lanes(x, op):
    out = x[:, 0:LANES]
    for c in range(1, x.shape[1] // LANES):
        out = op(out, x[:, c * LANES:(c + 1) * LANES])
    return out


def _softmax_tile(s_ref, p_ref, mx_ref, alpha_ref, m_ref, bias_rows, l_ref=None):
    nrows, tk = s_ref.shape
    for r0 in range(0, nrows, ROW_GROUP):
        rows = pl.ds(r0, ROW_GROUP)
        mx_ref[rows, :] = _fold_lanes(s_ref[rows, :] + bias_rows(r0), jnp.maximum)
    m_prev = m_ref[...]
    m_new = jnp.maximum(m_prev, jnp.max(mx_ref[...], axis=-1, keepdims=True))
    alpha_ref[...] = jnp.exp2(m_prev - m_new)
    m_ref[...] = m_new
    for r0 in range(0, nrows, ROW_GROUP):
        rows = pl.ds(r0, ROW_GROUP)
        s = s_ref[rows, :] + bias_rows(r0)
        p = jnp.exp2(s - jnp.concatenate([m_ref[rows, :]] * (tk // LANES), axis=1))
        if l_ref is not None:
            l_ref[rows, :] = alpha_ref[rows, :] * l_ref[rows, :] + _fold_lanes(p, jnp.add)
        p_ref[rows, :] = p.astype(BF16)


def _values_and_ones(v):
    return jnp.concatenate([v.astype(BF16), jnp.ones(v.shape, BF16)], axis=1)


def _rescale(alpha):
    return jnp.concatenate([alpha, alpha], axis=1)


def _softmax_reset(m_ref, acc_ref):
    m_ref[...] = jnp.full(m_ref.shape, NEG_INF, F32)
    acc_ref[...] = jnp.zeros(acc_ref.shape, F32)


def _diff_finish(vals, sums, lam, lam_init, subln_g, gate):
    t = vals.shape[0] // 2
    o = vals / sums
    o = o[:t] - lam * o[t:]
    o = o * lax.rsqrt(jnp.mean(o * o, axis=-1, keepdims=True) + SUBLN_EPS) * subln_g * (1.0 - lam_init)
    return o * (gate * _sigmoid(gate))


def _lambda(lq1_ref, lk1_ref, lq2_ref, lk2_ref, lam_init):
    s1 = jnp.sum(lq1_ref[...] * lk1_ref[...], axis=-1, keepdims=True)
    s2 = jnp.sum(lq2_ref[...] * lk2_ref[...], axis=-1, keepdims=True)
    return jnp.exp(s1) - jnp.exp(s2) + lam_init


def _local_bias(slope, nrows, nk, tq):
    i = lax.broadcasted_iota(jnp.int32, (nrows, nk), 0)
    i = jnp.where(i >= tq, i - tq, i)
    j = lax.broadcasted_iota(jnp.int32, (nrows, nk), 1)
    bias = slope * (i - jnp.abs(i - j)).astype(F32)
    shift = CHUNK.bit_length() - 1
    visible = jnp.right_shift(j, shift) <= jnp.right_shift(i, shift)
    return jnp.where(visible, bias, NEG_INF)


def _head_slope(h):
    return lax.bitcast_convert_type(jnp.full((1, 1), (126 - h) << 23, jnp.int32), F32) * LOG2E


def _attn_prompt_kernel(qi_ref, kj_ref, q_ref, k_ref, v_ref, gd_ref, lq1_ref, lk1_ref, lq2_ref, lk2_ref, sg_ref,
                        o_ref, q2_ref, s_ref, p_ref, bias_ref, mx_ref, alpha_ref, m_ref, l_ref, acc_ref,
                        *, tq, tk, lam_init):
    qi = qi_ref[pl.program_id(1)]
    kj = kj_ref[pl.program_id(1)]
    heads = [(n, slice(n * DV_DIFF, (n + 1) * DV_DIFF), _head_slope(pl.program_id(0) * HEADS_PER_STEP + n))
             for n in range(HEADS_PER_STEP)]

    @pl.when((qi == 0) & (kj == 0))
    def _():
        for n, _, slope in heads:
            bias_ref[n] = _local_bias(slope, tq, tk, tq)

    @pl.when(kj == 0)
    def _():
        for n, cols, _ in heads:
            q2_ref[n] = _stack_maps(q_ref[:, cols] * (DK_DIFF ** -0.5 * LOG2E))
        _softmax_reset(m_ref, acc_ref)
        l_ref[...] = jnp.zeros(l_ref.shape, F32)

    def update(n, bias_rows):
        s_ref[n] = _dot(q2_ref[n], k_ref[n], _NT)
        _softmax_tile(s_ref.at[n], p_ref.at[n], mx_ref.at[n], alpha_ref.at[n], m_ref.at[n], bias_rows, l_ref.at[n])
        acc_ref[n] = alpha_ref[n] * acc_ref[n] + _dot(p_ref[n], v_ref[n])

    @pl.when(kj < qi)
    def _():
        j = lax.broadcasted_iota(jnp.int32, (1, tk), 1)
        dist = (j - (qi - kj) * tq).astype(F32)
        for n, _, slope in heads:
            bias = slope * dist
            update(n, lambda r0: bias)

    @pl.when(kj == qi)
    def _():
        lam = _lambda(lq1_ref, lk1_ref, lq2_ref, lk2_ref, lam_init)
        for n, cols, _ in heads:
            update(n, lambda r0: bias_ref[n, pl.ds(r0 % tq, ROW_GROUP), :])
            l = jnp.sum(l_ref[n], axis=-1, keepdims=True)
            o_ref[:, cols] = _diff_finish(acc_ref[n], l, lam, lam_init, sg_ref[...],
                                          gd_ref[:, cols]).astype(o_ref.dtype)


def _attn_prompt(gq, k16, v16, lq1, lk1, lq2, lk2, subln_g, lam_init, tq):
    t = gq.shape[0]
    assert t % tq == 0 and tq % CHUNK == 0
    nq = t // tq
    hs = HEADS_PER_STEP
    wide = hs * DV_DIFF
    assert H_DIFF % hs == 0
    qcol, gcol = C_RWKV // wide, (C_RWKV + C_DIFF) // wide
    past = lambda: pl.BlockSpec((hs, tq, DV_DIFF), lambda h, s, qi, kj: (h, kj[s], 0))
    vec = lambda n: pl.BlockSpec((1, n), lambda h, s, qi, kj: (0, 0))
    per_head = lambda shape, dtype: pltpu.VMEM((hs,) + shape, dtype)
    kern = functools.partial(_attn_prompt_kernel, tq=tq, tk=tq, lam_init=lam_init)
    qi_tab = jnp.asarray([i for i in range(nq) for _ in range(i + 1)], jnp.int32)
    kj_tab = jnp.asarray([j for i in range(nq) for j in range(i + 1)], jnp.int32)
    grid_spec = pltpu.PrefetchScalarGridSpec(
        num_scalar_prefetch=2,
        grid=(H_DIFF // hs, nq * (nq + 1) // 2),
        in_specs=[
            pl.BlockSpec((tq, wide), lambda h, s, qi, kj: (qi[s], qcol + h)),
            past(), past(),
            pl.BlockSpec((tq, wide), lambda h, s, qi, kj: (qi[s], gcol + h)),
            vec(DK_DIFF), vec(DK_DIFF), vec(DK_DIFF), vec(DK_DIFF), vec(DV_DIFF),
        ],
        out_specs=pl.BlockSpec((tq, wide), lambda h, s, qi, kj: (qi[s], h)),
        scratch_shapes=[
            per_head((2 * tq, DV_DIFF), BF16),
            per_head((2 * tq, tq), F32),
            per_head((2 * tq, tq), BF16),
            per_head((tq, tq), F32),
            per_head((2 * tq, LANES), F32), per_head((2 * tq, LANES), F32),
            per_head((2 * tq, LANES), F32),
            per_head((2 * tq, LANES), F32),
            per_head((2 * tq, DV_DIFF), F32),
        ],
    )
    return pl.pallas_call(
        kern,
        grid_spec=grid_spec,
        out_shape=jax.ShapeDtypeStruct((t, C_DIFF), BF16),
        compiler_params=_cparams(("arbitrary", "arbitrary")),
        name="attn_prompt",
    )(qi_tab, kj_tab, gq, k16, v16, gq, lq1, lk1, lq2, lk2, subln_g)


def _attn_sample_kernel(q_ref, kn_ref, vn_ref, gd_ref, ck_ref, cv_ref, lq1_ref, lk1_ref, lq2_ref, lk2_ref, sg_ref,
                        o_ref, q2_ref, s_ref, p_ref, mx_ref, alpha_ref, m_ref, acc_ref,
                        *, t, tk, past_len, lam_init):
    kj = pl.program_id(1)
    nk = pl.num_programs(1) - 1
    heads = [(h, slice(h * DV_DIFF, (h + 1) * DV_DIFF), 2.0 ** -(h + 1) * LOG2E) for h in range(H_DIFF)]

    @pl.when(kj == 0)
    def _():
        for h, cols, _ in heads:
            q2_ref[h] = _stack_maps(q_ref[:, cols] * (DK_DIFF ** -0.5 * LOG2E))
        _softmax_reset(m_ref, acc_ref)

    @pl.when(kj < nk)
    def _():
        j = lax.broadcasted_iota(jnp.int32, (1, tk), 1)
        dist = (j + kj * tk - past_len).astype(F32)
        for h, cols, slope in heads:
            head_rows = pl.ds(h, tk, stride=H_DIFF)
            bias = slope * dist
            s_ref[h] = _dot(q2_ref[h], ck_ref[head_rows, :].astype(BF16), _NT)
            _softmax_tile(s_ref.at[h], p_ref.at[h], mx_ref.at[h], alpha_ref.at[h], m_ref.at[h], lambda r0: bias)
            acc_ref[h] = (_rescale(alpha_ref[h]) * acc_ref[h]
                          + _dot(p_ref[h], _values_and_ones(cv_ref[head_rows, :])))

    @pl.when(kj == nk)
    def _():
        lam = _lambda(lq1_ref, lk1_ref, lq2_ref, lk2_ref, lam_init)
        for h, cols, slope in heads:
            s = _dot(q2_ref[h], kn_ref[h], _NT) + _local_bias(slope, 2 * t, t, t)
            m_prev = m_ref[h]
            m_new = jnp.maximum(m_prev, jnp.max(s, axis=-1, keepdims=True))
            alpha = jnp.exp2(m_prev - m_new)
            p = jnp.exp2(s - m_new[:, 0:t]).astype(BF16)
            acc = _rescale(alpha) * acc_ref[h] + _dot(p, _values_and_ones(vn_ref[h]))
            o_ref[:, cols] = _diff_finish(acc[:, :DV_DIFF], acc[:, DV_DIFF:], lam, lam_init, sg_ref[...],
                                          gd_ref[:, cols]).astype(o_ref.dtype)


def _attn_sample(gq, k16, v16, cache_k, cache_v, layer, lq1, lk1, lq2, lk2, subln_g, lam_init, tk):
    depth, bsz, past_len = cache_k.shape[:3]
    cache_k = cache_k.reshape(depth, bsz, past_len * H_DIFF, DV_DIFF)
    cache_v = cache_v.reshape(depth, bsz, past_len * H_DIFF, DV_DIFF)
    t = gq.shape[0] // bsz
    assert t <= CHUNK and past_len % CHUNK == 0 and past_len % tk == 0
    nk = past_len // tk
    vec = lambda n: pl.BlockSpec((1, n), lambda b, j: (0, 0))
    new = lambda col: pl.BlockSpec((t, C_DIFF), lambda b, j: (b, col))
    fresh = pl.BlockSpec((H_DIFF, t, DV_DIFF), lambda b, j: (0, b, 0))
    past = pl.BlockSpec((None, None, tk * H_DIFF, DV_DIFF),
                        lambda b, j: (layer, b, jnp.minimum(j, nk - 1), 0))
    kern = functools.partial(_attn_sample_kernel, t=t, tk=tk, past_len=past_len, lam_init=lam_init)
    per_head = lambda shape, dtype: pltpu.VMEM((H_DIFF,) + shape, dtype)
    return pl.pallas_call(
        kern,
        grid=(bsz, nk + 1),
        in_specs=[new(1), fresh, fresh, new(2), past, past,
                  vec(DK_DIFF), vec(DK_DIFF), vec(DK_DIFF), vec(DK_DIFF), vec(DV_DIFF)],
        out_specs=pl.BlockSpec((t, C_DIFF), lambda b, j: (b, 0)),
        out_shape=jax.ShapeDtypeStruct((bsz * t, C_DIFF), BF16),
        scratch_shapes=[
            per_head((2 * t, DV_DIFF), BF16),
            per_head((2 * t, tk), F32),
            per_head((2 * t, tk), BF16),
            per_head((2 * t, LANES), F32), per_head((2 * t, LANES), F32),
            per_head((2 * t, LANES), F32),
            per_head((2 * t, 2 * DV_DIFF), F32),
        ],
        compiler_params=_cparams(("parallel", "arbitrary")),
        name="attn_sample",
    )(gq, k16, v16, gq, cache_k, cache_v, lq1, lk1, lq2, lk2, subln_g)


def kernel(x_prompt, x_sample, cache_k, cache_v, state_wkv, state_shift, norm_g, w_in, shift_mu, w_decay0, w_decay2,
           a0, w_a2, k_k, k_a, r_k, lnx_w, lnx_b, lam_q1, lam_k1, lam_q2, lam_k2, subln_g, w_out, final_g):
    depth = w_in.shape[0]
    bp, tp, d = x_prompt.shape
    bs, ts, _ = x_sample.shape
    assert bp == 1, "the prompt attention kernel handles one sequence"

    xp = x_prompt.reshape(bp * tp, d)
    xs = x_sample.reshape(bs * ts, d)
    w_in_b = w_in.astype(BF16)
    w_out_b = w_out.astype(BF16)
    zero_state = jnp.zeros((bp, H_RWKV, RWKV_HEAD, RWKV_HEAD), state_wkv.dtype)
    zero_shift = jnp.zeros((bp, 1, D_SHIFT), state_shift.dtype)
    row = lambda a: a.reshape(1, -1)

    outs = {name: [] for name in ("sp", "hp", "ss", "hs")}
    kv_p = [jnp.zeros((depth, bp * tp, C_DIFF), F32) for _ in range(2)]
    kv_s = [jnp.zeros((depth, bs * ts, C_DIFF), F32) for _ in range(2)]
    for l in range(depth):
        lam_init = _lambda_init(l)
        g = row(norm_g[l])
        w_sh = w_in_b[l, :, :D_SHIFT]
        w_rest = w_in_b[l, :, D_SHIFT:]
        rw = (row(shift_mu[l]), row(w_decay0[l]), w_decay2[l], row(a0[l]), w_a2[l], row(k_k[l]), row(k_a[l]),
              row(r_k[l]), row(lnx_w[l]), row(lnx_b[l]))
        lam = (row(lam_q1[l]), row(lam_k1[l]), row(lam_q2[l]), row(lam_k2[l]), row(subln_g[l]))

        def stream(x, bsz, kv, s0, prev0, attn):
            sh = _norm_matmul(x, g, w_sh, 1024, 640)
            gq, kv[0], kv[1], k16, v16 = _proj_rest(x, g, w_rest, kv[0], kv[1], l, 512)
            y_r, s_t = _rwkv_mix(sh, gq, s0, prev0, *rw)
            y_d = attn(gq, k16, v16)
            x = _out_matmul(x, y_r, y_d, w_out_b[l], 512, 1024)
            shift = sh.reshape(bsz, sh.shape[0] // bsz, D_SHIFT)[:, -1]
            return x, s_t, shift

        xp, sn, hn = stream(
            xp, bp, kv_p, zero_state, zero_shift,
            lambda gq, k16, v16: _attn_prompt(gq, k16, v16, *lam, lam_init, 512))
        outs["sp"].append(sn); outs["hp"].append(hn)
        xs, sn, hn = stream(
            xs, bs, kv_s, state_wkv[l], state_shift[l].reshape(bs, 1, D_SHIFT),
            lambda gq, k16, v16: _attn_sample(gq, k16, v16, cache_k, cache_v, l, *lam, lam_init, 1024))
        outs["ss"].append(sn); outs["hs"].append(hn)

    yp = _final_norm(xp, row(final_g), 512).reshape(bp, tp, d)
    ys = _final_norm(xs, row(final_g), 512).reshape(bs, ts, d)
    st = lambda name: jnp.stack(outs[name])
    heads = lambda a, bsz, t: a.reshape(depth, bsz, t, H_DIFF, DV_DIFF)
    return (yp, ys, heads(kv_p[0], bp, tp), heads(kv_p[1], bp, tp), st("sp"), st("hp"),
            heads(kv_s[0], bs, ts), heads(kv_s[1], bs, ts), st("ss"), st("hs"))
```

```python
import functools
import math

import jax
import jax.numpy as jnp
from jax import lax
from jax.experimental import pallas as pl
from jax.experimental.pallas import tpu as pltpu

F32 = jnp.float32
BF16 = jnp.bfloat16

D_MODEL = 2048
CHUNK = 64
C_RWKV = 1024
RWKV_HEAD = 64
H_RWKV = C_RWKV // RWKV_HEAD
N_PAIR = H_RWKV // 2
D_LORA = 64
D_SHIFT = 3 * C_RWKV + 2 * D_LORA
C_DIFF = 1024
H_DIFF = 8
DV_DIFF = 128
DK_DIFF = 64
D_REST = C_RWKV + 4 * C_DIFF
NORM_EPS = 1e-6
RWKV_LN_EPS = 64e-5
SUBLN_EPS = 1e-5
NEG_INF = -1e30
LANES = 128
L_CHUNK = 64
EXP_M05 = math.exp(-0.5)
LOG2E = math.log2(math.e)
ROW_GROUP = 16
HEADS_PER_STEP = 4
VMEM_LIMIT = 48 * 1024 * 1024


def _lambda_init(l):
    return 0.8 - 0.6 * math.exp(-0.3 * l)


def _cparams(sem, vmem_limit=VMEM_LIMIT):
    return pltpu.CompilerParams(dimension_semantics=sem, vmem_limit_bytes=vmem_limit)


_NN = (((1,), (0,)), ((), ()))
_NT = (((1,), (1,)), ((), ()))
_TN = (((0,), (0,)), ((), ()))
_BNN = (((2,), (1,)), ((0,), (0,)))
_BNT = (((2,), (2,)), ((0,), (0,)))
_BTN = (((1,), (1,)), ((0,), (0,)))


def _dot(a, b, dims=_NN):
    return lax.dot_general(a, b, dims, preferred_element_type=F32)


def _split2(x):
    hi = x.astype(BF16)
    lo = (x - hi.astype(F32)).astype(BF16)
    return hi, lo


def _split3(x):
    hi = x.astype(BF16)
    r1 = x - hi.astype(F32)
    mid = r1.astype(BF16)
    lo = (r1 - mid.astype(F32)).astype(BF16)
    return hi, mid, lo


def _mm3(a, b, dims=_NN):
    ah, al = _split2(a)
    bh, bl = _split2(b)
    return _dot(ah, bh, dims) + (_dot(ah, bl, dims) + _dot(al, bh, dims))


def _mm_exact_lhs(a_bf16, b, dims=_NN):
    bh, bm, bl = _split3(b)
    return _dot(a_bf16, bh, dims) + (_dot(a_bf16, bm, dims) + _dot(a_bf16, bl, dims))


def _sum_heads(a, e2):
    return _dot(a.astype(BF16), e2)


def _sigmoid(x):
    return 1.0 / (1.0 + jnp.exp(-x))


def _norm_matmul_kernel(x_ref, g_ref, w_ref, o_ref, h_ref):
    @pl.when(pl.program_id(1) == 0)
    def _():
        x = x_ref[...]
        ms = jnp.mean(x * x, axis=-1, keepdims=True)
        h_ref[...] = (x * lax.rsqrt(ms + NORM_EPS) * g_ref[...]).astype(BF16)

    o_ref[...] = jnp.dot(h_ref[...], w_ref[...], preferred_element_type=F32)


def _norm_matmul(x, g, w, layer, tm, tn):
    m, d = x.shape
    n = w.shape[2]
    return pl.pallas_call(
        _norm_matmul_kernel,
        grid=(m // tm, n // tn),
        in_specs=[
            pl.BlockSpec((tm, d), lambda i, j: (i, 0)),
            pl.BlockSpec((1, d), lambda i, j: (0, 0)),
            pl.BlockSpec((None, d, tn), lambda i, j: (layer, 0, j)),
        ],
        out_specs=pl.BlockSpec((tm, tn), lambda i, j: (i, j)),
        out_shape=jax.ShapeDtypeStruct((m, n), F32),
        scratch_shapes=[pltpu.VMEM((tm, d), BF16)],
        compiler_params=_cparams(("parallel", "arbitrary")),
        name="norm_matmul",
    )(x, g, w)


def _proj_rest_kernel(x_ref, g_ref, w_ref, *refs):
    o_ref, k_ref, v_ref, k16_ref, v16_ref, h_ref = refs[-6:]
    j = pl.program_id(1)

    @pl.when(j == 0)
    def _():
        x = x_ref[...]
        ms = jnp.mean(x * x, axis=-1, keepdims=True)
        h_ref[...] = (x * lax.rsqrt(ms + NORM_EPS) * g_ref[...]).astype(BF16)

    acc = jnp.dot(h_ref[...], w_ref[...], preferred_element_type=F32)

    @pl.when((j != _KCOL) & (j != _VCOL))
    def _():
        o_ref[...] = acc

    def by_head(ref16):
        for h in range(H_DIFF):
            ref16[h] = acc[:, h * DV_DIFF:(h + 1) * DV_DIFF].astype(BF16)

    @pl.when(j == _KCOL)
    def _():
        k_ref[...] = acc
        by_head(k16_ref)

    @pl.when(j == _VCOL)
    def _():
        v_ref[...] = acc
        by_head(v16_ref)


_KCOL, _VCOL = 2, 3


def _proj_rest(x, g, w, kv_all, layer, tm):
    m, d = x.shape
    depth = w.shape[0]
    tn = C_DIFF
    assert w.shape[2] == 5 * tn and C_RWKV == tn
    ocol = lambda j: jnp.minimum(j, 1) + j // 4
    carried = () if kv_all is None else tuple(kv_all)
    return pl.pallas_call(
        _proj_rest_kernel,
        grid=(m // tm, 5),
        in_specs=[
            pl.BlockSpec((tm, d), lambda i, j: (i, 0)),
            pl.BlockSpec((1, d), lambda i, j: (0, 0)),
            pl.BlockSpec((None, d, tn), lambda i, j: (layer, 0, j)),
        ] + [pl.BlockSpec(memory_space=pl.ANY) for _ in carried],
        out_specs=[
            pl.BlockSpec((tm, tn), lambda i, j: (i, ocol(j))),
            pl.BlockSpec((None, tm, tn), lambda i, j: (layer, i, 0)),
            pl.BlockSpec((None, tm, tn), lambda i, j: (layer, i, 0)),
            pl.BlockSpec((H_DIFF, tm, DV_DIFF), lambda i, j: (0, i, 0)),
            pl.BlockSpec((H_DIFF, tm, DV_DIFF), lambda i, j: (0, i, 0)),
        ],
        out_shape=[
            jax.ShapeDtypeStruct((m, 3 * tn), F32),
            jax.ShapeDtypeStruct((depth, m, tn), F32),
            jax.ShapeDtypeStruct((depth, m, tn), F32),
            jax.ShapeDtypeStruct((H_DIFF, m, DV_DIFF), BF16),
            jax.ShapeDtypeStruct((H_DIFF, m, DV_DIFF), BF16),
        ],
        input_output_aliases={3: 1, 4: 2} if carried else {},
        scratch_shapes=[pltpu.VMEM((tm, d), BF16)],
        compiler_params=_cparams(("parallel", "arbitrary"),
                                 8 * tm * d + 4 * d * tn + 32 * tm * tn + 2 * tm * d + (4 << 20)),
        name="proj_rest",
    )(x, g, w, *carried)


def _final_norm_kernel(x_ref, g_ref, o_ref):
    x = x_ref[...]
    ms = jnp.mean(x * x, axis=-1, keepdims=True)
    o_ref[...] = x * lax.rsqrt(ms + NORM_EPS) * g_ref[...]


def _final_norm(x, g, tm):
    m, d = x.shape
    return pl.pallas_call(
        _final_norm_kernel,
        grid=(m // tm,),
        in_specs=[pl.BlockSpec((tm, d), lambda i: (i, 0)), pl.BlockSpec((1, d), lambda i: (0, 0))],
        out_specs=pl.BlockSpec((tm, d), lambda i: (i, 0)),
        out_shape=jax.ShapeDtypeStruct((m, d), F32),
        compiler_params=_cparams(("parallel",)),
        name="final_norm",
    )(x, g)


def _out_matmul_kernel(x_ref, yr_ref, yd_ref, wr_ref, wd_ref, o_ref):
    acc = jnp.dot(yr_ref[...], wr_ref[...], preferred_element_type=F32)
    acc = acc + jnp.dot(yd_ref[...], wd_ref[...], preferred_element_type=F32)
    o_ref[...] = x_ref[...] + acc


def _out_matmul(x, y_r, y_d, w_out, layer, tm, tn):
    m, d = x.shape
    return pl.pallas_call(
        _out_matmul_kernel,
        grid=(m // tm, d // tn),
        in_specs=[
            pl.BlockSpec((tm, tn), lambda i, j: (i, j)),
            pl.BlockSpec((tm, C_RWKV), lambda i, j: (i, 0)),
            pl.BlockSpec((tm, C_DIFF), lambda i, j: (i, 0)),
            pl.BlockSpec((None, C_RWKV, tn), lambda i, j: (layer, 0, j)),
            pl.BlockSpec((None, C_DIFF, tn), lambda i, j: (layer, 1, j)),
        ],
        out_specs=pl.BlockSpec((tm, tn), lambda i, j: (i, j)),
        out_shape=jax.ShapeDtypeStruct((m, d), F32),
        compiler_params=_cparams(("parallel", "parallel")),
        name="out_matmul",
    )(x, y_r, y_d, w_out, w_out)


def _rwkv_kernel(sh_ref, gr_ref, s0_ref, p0_ref, mu_ref, wd0_ref, wd2_ref, a0_ref, wa2_ref,
                 kk_ref, ka_ref, rk_ref, lnw_ref, lnb_ref,
                 y_ref, st_ref,
                 bd_ref, prev_ref, *, nsub):
    c = pl.program_id(1)
    nc = pl.num_programs(1)
    L = L_CHUNK
    R = nsub * L
    C = C_RWKV

    @pl.when(c == 0)
    def _():
        prev_ref[...] = p0_ref[0]
        for p in range(N_PAIR):
            bd_ref[p] = jnp.zeros((LANES, LANES), F32)
            bd_ref[p, 0:RWKV_HEAD, 0:RWKV_HEAD] = s0_ref[0, 2 * p]
            bd_ref[p, RWKV_HEAD:LANES, RWKV_HEAD:LANES] = s0_ref[0, 2 * p + 1]

    sh = sh_ref[...]
    row0 = lax.broadcasted_iota(jnp.int32, sh.shape, 0) == 0
    prev = jnp.where(row0, prev_ref[...], pltpu.roll(sh, 1, 0))
    xs = sh + (prev - sh) * mu_ref[...]
    prev_ref[...] = sh_ref[R - 1:R, :]

    r = xs[:, 0:C]
    kr = xs[:, C:2 * C]
    v = xs[:, 2 * C:3 * C]
    la = xs[:, 3 * C:3 * C + LANES]

    lane = lax.broadcasted_iota(jnp.int32, (L, LANES), 1)
    head0 = lane < RWKV_HEAD
    w2 = jnp.concatenate([wd2_ref[...], wa2_ref[...]], axis=0)
    w_lo = lax.broadcasted_iota(jnp.int32, la.shape, 1) < D_LORA
    dw = _mm3(jnp.where(w_lo, jnp.tanh(la), 0.0), w2)
    da = _mm3(jnp.where(w_lo, 0.0, la), w2)
    logw = -EXP_M05 * _sigmoid(wd0_ref[...] + dw)
    a = _sigmoid(a0_ref[...] + da)

    er = lax.broadcasted_iota(jnp.int32, (LANES, LANES), 0)
    ec = lax.broadcasted_iota(jnp.int32, (LANES, LANES), 1)
    e2 = jnp.where((er < RWKV_HEAD) == (ec < RWKV_HEAD), 1.0, 0.0).astype(BF16)

    def head_sums(x):
        return jnp.concatenate(
            [_sum_heads(x[:, p * LANES:(p + 1) * LANES], e2) for p in range(N_PAIR)], axis=1)

    kk = kr * kk_ref[...]
    kk = kk / jnp.maximum(jnp.sqrt(head_sums(kk * kk)), 1e-12)
    kw = kr * (1.0 + (a - 1.0) * ka_ref[...])
    b = kk * a

    tr = lax.broadcasted_iota(jnp.int32, (R, R), 0)
    tc = lax.broadcasted_iota(jnp.int32, (R, R), 1)
    lbits = L.bit_length() - 1
    tril = jnp.where((tr >= tc) & (jnp.right_shift(tr, lbits) == jnp.right_shift(tc, lbits)), 1.0, 0.0).astype(BF16)
    cum = _mm_exact_lhs(tril, logw)
    cum_l = [cum[n * L + L - 1:(n + 1) * L, :] for n in range(nsub)]
    gi = jnp.exp(-cum)
    gl = jnp.exp(jnp.concatenate([jnp.broadcast_to(cl, (L, C)) for cl in cum_l], axis=0) - cum)
    srow = lax.broadcasted_iota(jnp.int32, (L, LANES), 0)
    scol = jnp.where(head0, lane, lane - RWKV_HEAD)
    strict = srow > scol
    incl = srow >= scol
    eye = jnp.where(srow == scol, 1.0, 0.0)
    same_head = (er < RWKV_HEAD) == (ec < RWKV_HEAD)

    def side(x):
        return jnp.stack([x[n * L:(n + 1) * L, p * LANES:(p + 1) * LANES]
                          for n in range(nsub) for p in range(N_PAIR)])

    def expand(xs):
        return jnp.concatenate([jnp.where(head0, xs, 0.0), jnp.where(head0, 0.0, xs)], axis=1).astype(BF16)

    v_s = side(v).astype(BF16)
    vb = expand(side(v))
    lhs = jnp.concatenate([side(kk * jnp.exp(cum - logw)), side(r * jnp.exp(cum))], axis=1).astype(BF16)
    gram = _dot(lhs, jnp.concatenate([expand(side(kw * gi)), expand(side(b * gi))], axis=1), _BNT)
    a_kk = jnp.where(strict, gram[:, :L, :2 * L], 0.0)
    a_kb = jnp.where(strict, gram[:, :L, 2 * L:], 0.0)
    a_rk = jnp.where(incl, gram[:, L:, :2 * L], 0.0)
    a_rb = jnp.where(incl, gram[:, L:, 2 * L:], 0.0)
    t = eye - a_kb
    q = _dot((-a_kb).astype(BF16), expand(-a_kb), _BNN)
    for _ in range(4):
        both = _dot(q.astype(BF16), jnp.concatenate([expand(q), expand(t)], axis=2), _BNN)
        q = both[:, :, :2 * L]
        t = t + both[:, :, 2 * L:]
    t = t + _dot(q.astype(BF16), expand(t), _BNN)
    t = t.astype(BF16)
    a_kk = a_kk.astype(BF16)
    a_r = jnp.concatenate([a_rk, -a_rb], axis=2).astype(BF16)
    kbh = jnp.concatenate([side(kw * gl), -side(b * gl)], axis=1).astype(BF16)
    state = bd_ref[...]
    y_rows = []
    for n in range(nsub):
        sub = slice(n * N_PAIR, (n + 1) * N_PAIR)
        zy = _dot(lhs[sub], state.astype(BF16), _BNT)
        z = zy[:, :L] + _dot(a_kk[sub], vb[sub], _BNN)
        u = _dot(t[sub], expand(z), _BNN)
        ys = zy[:, L:] + _dot(a_r[sub], jnp.concatenate([vb[sub], expand(u)], axis=1), _BNN)
        g_l = jnp.exp(cum_l[n])
        decay = jnp.stack([g_l[:, p * LANES:(p + 1) * LANES] for p in range(N_PAIR)])
        grow = _dot(jnp.concatenate([v_s[sub], u.astype(BF16)], axis=1), kbh[sub], _BTN)
        state = state * decay + jnp.where(same_head, grow, 0.0)
        y_rows.append(jnp.concatenate([ys[p] for p in range(N_PAIR)], axis=1))
    bd_ref[...] = state
    y = jnp.concatenate(y_rows, axis=0)

    d = y - head_sums(y) * (1.0 / RWKV_HEAD)
    var = head_sums(d * d) * (1.0 / RWKV_HEAD)
    yn = d * lax.rsqrt(var + RWKV_LN_EPS) * lnw_ref[...] + lnb_ref[...]
    bonus = head_sums(r * kw * rk_ref[...]) * v
    g = gr_ref[...]
    y_ref[...] = ((yn + bonus) * (g * _sigmoid(g))).astype(y_ref.dtype)

    @pl.when(c == nc - 1)
    def _():
        for p in range(N_PAIR):
            st_ref[0, 2 * p] = bd_ref[p, 0:RWKV_HEAD, 0:RWKV_HEAD]
            st_ref[0, 2 * p + 1] = bd_ref[p, RWKV_HEAD:LANES, RWKV_HEAD:LANES]


def _rwkv_mix(sh, rest, s0, prev0, mu, wd0, wd2, a0, wa2, k_k, k_a, r_k, lnw, lnb):
    bsz = s0.shape[0]
    t = sh.shape[0] // bsz
    assert t % L_CHUNK == 0
    nsub = max(n for n in (1, 2, 4) if t % (n * L_CHUNK) == 0)
    L = nsub * L_CHUNK
    nc = t // L
    row = lambda b, c: (b * nc + c, 0)
    vec = lambda n: pl.BlockSpec((1, n), lambda b, c: (0, 0))
    full = lambda shape: pltpu.VMEM(shape, F32)
    return pl.pallas_call(
        functools.partial(_rwkv_kernel, nsub=nsub),
        grid=(bsz, nc),
        in_specs=[
            pl.BlockSpec((L, D_SHIFT), row),
            pl.BlockSpec((L, C_RWKV), row),
            pl.BlockSpec((1, H_RWKV, RWKV_HEAD, RWKV_HEAD), lambda b, c: (b, 0, 0, 0)),
            pl.BlockSpec((1, 1, D_SHIFT), lambda b, c: (b, 0, 0)),
            vec(D_SHIFT), vec(C_RWKV),
            pl.BlockSpec((D_LORA, C_RWKV), lambda b, c: (0, 0)),
            vec(C_RWKV),
            pl.BlockSpec((D_LORA, C_RWKV), lambda b, c: (0, 0)),
            vec(C_RWKV), vec(C_RWKV), vec(C_RWKV), vec(C_RWKV), vec(C_RWKV),
        ],
        out_specs=[
            pl.BlockSpec((L, C_RWKV), row),
            pl.BlockSpec((1, H_RWKV, RWKV_HEAD, RWKV_HEAD), lambda b, c: (b, 0, 0, 0)),
        ],
        out_shape=[
            jax.ShapeDtypeStruct((bsz * t, C_RWKV), BF16),
            jax.ShapeDtypeStruct((bsz, H_RWKV, RWKV_HEAD, RWKV_HEAD), F32),
        ],
        scratch_shapes=[
            full((N_PAIR, LANES, LANES)), full((1, D_SHIFT)),
        ],
        compiler_params=_cparams(("parallel", "arbitrary")),
        name="rwkv_mix",
    )(sh, rest, s0, prev0, mu, wd0, wd2, a0, wa2, k_k, k_a, r_k, lnw, lnb)


def _stack_maps(q):
    lane = lax.broadcasted_iota(jnp.int32, q.shape, 1)
    m0 = lane < DK_DIFF
    return jnp.concatenate([jnp.where(m0, q, 0.0), jnp.where(m0, 0.0, q)], axis=0).astype(BF16)


def _fold_lanes(x, op):
    out = x[:, 0:LANES]
    for c in range(1, x.shape[1] // LANES):
        out = op(out, x[:, c * LANES:(c + 1) * LANES])
    return out


def _softmax_tile(s_ref, p_ref, mx_ref, alpha_ref, m_ref, bias_rows, l_ref=None):
    nrows, tk = s_ref.shape
    for r0 in range(0, nrows, ROW_GROUP):
        rows = pl.ds(r0, ROW_GROUP)
        mx_ref[rows, :] = _fold_lanes(s_ref[rows, :] + bias_rows(r0), jnp.maximum)
    m_prev = m_ref[...]
    m_new = jnp.maximum(m_prev, jnp.max(mx_ref[...], axis=-1, keepdims=True))
    alpha_ref[...] = jnp.exp2(m_prev - m_new)
    m_ref[...] = m_new
    for r0 in range(0, nrows, ROW_GROUP):
        rows = pl.ds(r0, ROW_GROUP)
        s = s_ref[rows, :] + bias_rows(r0)
        p = jnp.exp2(s - jnp.concatenate([m_ref[rows, :]] * (tk // LANES), axis=1))
        if l_ref is not None:
            l_ref[rows, :] = alpha_ref[rows, :] * l_ref[rows, :] + _fold_lanes(p, jnp.add)
        p_ref[rows, :] = p.astype(BF16)


def _values_and_ones(v):
    return jnp.concatenate([v.astype(BF16), jnp.ones(v.shape, BF16)], axis=1)


def _rescale(alpha):
    return jnp.concatenate([alpha, alpha], axis=1)


def _softmax_reset(m_ref, acc_ref):
    m_ref[...] = jnp.full(m_ref.shape, NEG_INF, F32)
    acc_ref[...] = jnp.zeros(acc_ref.shape, F32)


def _diff_finish(vals, sums, lam, lam_init, subln_g, gate):
    t = vals.shape[0] // 2
    o = vals / sums
    o = o[:t] - lam * o[t:]
    o = o * lax.rsqrt(jnp.mean(o * o, axis=-1, keepdims=True) + SUBLN_EPS) * subln_g * (1.0 - lam_init)
    return o * (gate * _sigmoid(gate))


def _lambda(lq1_ref, lk1_ref, lq2_ref, lk2_ref, lam_init):
    s1 = jnp.sum(lq1_ref[...] * lk1_ref[...], axis=-1, keepdims=True)
    s2 = jnp.sum(lq2_ref[...] * lk2_ref[...], axis=-1, keepdims=True)
    return jnp.exp(s1) - jnp.exp(s2) + lam_init


def _local_bias(slope, nrows, nk, tq):
    i = lax.broadcasted_iota(jnp.int32, (nrows, nk), 0)
    i = jnp.where(i >= tq, i - tq, i)
    j = lax.broadcasted_iota(jnp.int32, (nrows, nk), 1)
    bias = slope * (i - jnp.abs(i - j)).astype(F32)
    shift = CHUNK.bit_length() - 1
    visible = jnp.right_shift(j, shift) <= jnp.right_shift(i, shift)
    return jnp.where(visible, bias, NEG_INF)


def _head_slope(h):
    return lax.bitcast_convert_type(jnp.full((1, 1), (126 - h) << 23, jnp.int32), F32) * LOG2E


def _attn_prompt_kernel(qi_ref, kj_ref, q_ref, k_ref, v_ref, gd_ref, lq1_ref, lk1_ref, lq2_ref, lk2_ref, sg_ref,
                        o_ref, q2_ref, s_ref, p_ref, bias_ref, mx_ref, alpha_ref, m_ref, l_ref, acc_ref,
                        *, tq, tk, lam_init):
    qi = qi_ref[pl.program_id(1)]
    kj = kj_ref[pl.program_id(1)]
    heads = [(n, slice(n * DV_DIFF, (n + 1) * DV_DIFF), _head_slope(pl.program_id(0) * HEADS_PER_STEP + n))
             for n in range(HEADS_PER_STEP)]

    @pl.when((qi == 0) & (kj == 0))
    def _():
        for n, _, slope in heads:
            bias_ref[n] = _local_bias(slope, tq, tk, tq)

    @pl.when(kj == 0)
    def _():
        for n, cols, _ in heads:
            q2_ref[n] = _stack_maps(q_ref[:, cols] * (DK_DIFF ** -0.5 * LOG2E))
        _softmax_reset(m_ref, acc_ref)
        l_ref[...] = jnp.zeros(l_ref.shape, F32)

    def update(n, bias_rows):
        s_ref[n] = _dot(q2_ref[n], k_ref[n], _NT)
        _softmax_tile(s_ref.at[n], p_ref.at[n], mx_ref.at[n], alpha_ref.at[n], m_ref.at[n], bias_rows, l_ref.at[n])
        acc_ref[n] = alpha_ref[n] * acc_ref[n] + _dot(p_ref[n], v_ref[n])

    @pl.when(kj < qi)
    def _():
        j = lax.broadcasted_iota(jnp.int32, (1, tk), 1)
        dist = (j - (qi - kj) * tq).astype(F32)
        for n, _, slope in heads:
            bias = slope * dist
            update(n, lambda r0: bias)

    @pl.when(kj == qi)
    def _():
        lam = _lambda(lq1_ref, lk1_ref, lq2_ref, lk2_ref, lam_init)
        for n, cols, _ in heads:
            update(n, lambda r0: bias_ref[n, pl.ds(r0 % tq, ROW_GROUP), :])
            l = jnp.sum(l_ref[n], axis=-1, keepdims=True)
            o_ref[:, cols] = _diff_finish(acc_ref[n], l, lam, lam_init, sg_ref[...],
                                          gd_ref[:, cols]).astype(o_ref.dtype)


def _attn_prompt(gq, k16, v16, lq1, lk1, lq2, lk2, subln_g, lam_init, tq):
    t = gq.shape[0]
    assert t % tq == 0 and tq % CHUNK == 0
    nq = t // tq
    hs = HEADS_PER_STEP
    wide = hs * DV_DIFF
    assert H_DIFF % hs == 0
    qcol, gcol = C_RWKV // wide, (C_RWKV + C_DIFF) // wide
    past = lambda: pl.BlockSpec((hs, tq, DV_DIFF), lambda h, s, qi, kj: (h, kj[s], 0))
    vec = lambda n: pl.BlockSpec((1, n), lambda h, s, qi, kj: (0, 0))
    per_head = lambda shape, dtype: pltpu.VMEM((hs,) + shape, dtype)
    kern = functools.partial(_attn_prompt_kernel, tq=tq, tk=tq, lam_init=lam_init)
    qi_tab = jnp.asarray([i for i in range(nq) for _ in range(i + 1)], jnp.int32)
    kj_tab = jnp.asarray([j for i in range(nq) for j in range(i + 1)], jnp.int32)
    grid_spec = pltpu.PrefetchScalarGridSpec(
        num_scalar_prefetch=2,
        grid=(H_DIFF // hs, nq * (nq + 1) // 2),
        in_specs=[
            pl.BlockSpec((tq, wide), lambda h, s, qi, kj: (qi[s], qcol + h)),
            past(), past(),
            pl.BlockSpec((tq, wide), lambda h, s, qi, kj: (qi[s], gcol + h)),
            vec(DK_DIFF), vec(DK_DIFF), vec(DK_DIFF), vec(DK_DIFF), vec(DV_DIFF),
        ],
        out_specs=pl.BlockSpec((tq, wide), lambda h, s, qi, kj: (qi[s], h)),
        scratch_shapes=[
            per_head((2 * tq, DV_DIFF), BF16),
            per_head((2 * tq, tq), F32),
            per_head((2 * tq, tq), BF16),
            per_head((tq, tq), F32),
            per_head((2 * tq, LANES), F32), per_head((2 * tq, LANES), F32),
            per_head((2 * tq, LANES), F32),
            per_head((2 * tq, LANES), F32),
            per_head((2 * tq, DV_DIFF), F32),
        ],
    )
    return pl.pallas_call(
        kern,
        grid_spec=grid_spec,
        out_shape=jax.ShapeDtypeStruct((t, C_DIFF), BF16),
        compiler_params=_cparams(("arbitrary", "arbitrary")),
        name="attn_prompt",
    )(qi_tab, kj_tab, gq, k16, v16, gq, lq1, lk1, lq2, lk2, subln_g)


def _attn_sample_kernel(q_ref, kn_ref, vn_ref, gd_ref, ck_ref, cv_ref, lq1_ref, lk1_ref, lq2_ref, lk2_ref, sg_ref,
                        o_ref, q2_ref, s_ref, p_ref, mx_ref, alpha_ref, m_ref, acc_ref,
                        *, t, tk, past_len, lam_init):
    kj = pl.program_id(1)
    nk = pl.num_programs(1) - 1
    heads = [(h, slice(h * DV_DIFF, (h + 1) * DV_DIFF), 2.0 ** -(h + 1) * LOG2E) for h in range(H_DIFF)]

    @pl.when(kj == 0)
    def _():
        for h, cols, _ in heads:
            q2_ref[h] = _stack_maps(q_ref[:, cols] * (DK_DIFF ** -0.5 * LOG2E))
        _softmax_reset(m_ref, acc_ref)

    @pl.when(kj < nk)
    def _():
        j = lax.broadcasted_iota(jnp.int32, (1, tk), 1)
        dist = (j + kj * tk - past_len).astype(F32)
        for h, cols, slope in heads:
            head_rows = pl.ds(h, tk, stride=H_DIFF)
            bias = slope * dist
            s_ref[h] = _dot(q2_ref[h], ck_ref[head_rows, :].astype(BF16), _NT)
            _softmax_tile(s_ref.at[h], p_ref.at[h], mx_ref.at[h], alpha_ref.at[h], m_ref.at[h], lambda r0: bias)
            acc_ref[h] = (_rescale(alpha_ref[h]) * acc_ref[h]
                          + _dot(p_ref[h], _values_and_ones(cv_ref[head_rows, :])))

    @pl.when(kj == nk)
    def _():
        lam = _lambda(lq1_ref, lk1_ref, lq2_ref, lk2_ref, lam_init)
        for h, cols, slope in heads:
            s = _dot(q2_ref[h], kn_ref[h], _NT) + _local_bias(slope, 2 * t, t, t)
            m_prev = m_ref[h]
            m_new = jnp.maximum(m_prev, jnp.max(s, axis=-1, keepdims=True))
            alpha = jnp.exp2(m_prev - m_new)
            p = jnp.exp2(s - m_new[:, 0:t]).astype(BF16)
            acc = _rescale(alpha) * acc_ref[h] + _dot(p, _values_and_ones(vn_ref[h]))
            o_ref[:, cols] = _diff_finish(acc[:, :DV_DIFF], acc[:, DV_DIFF:], lam, lam_init, sg_ref[...],
                                          gd_ref[:, cols]).astype(o_ref.dtype)


def _attn_sample(gq, k16, v16, cache_k, cache_v, layer, lq1, lk1, lq2, lk2, subln_g, lam_init, tk):
    depth, bsz, past_len = cache_k.shape[:3]
    cache_k = cache_k.reshape(depth, bsz, past_len * H_DIFF, DV_DIFF)
    cache_v = cache_v.reshape(depth, bsz, past_len * H_DIFF, DV_DIFF)
    t = gq.shape[0] // bsz
    assert t <= CHUNK and past_len % CHUNK == 0 and past_len % tk == 0
    nk = past_len // tk
    vec = lambda n: pl.BlockSpec((1, n), lambda b, j: (0, 0))
    new = lambda col: pl.BlockSpec((t, C_DIFF), lambda b, j: (b, col))
    fresh = pl.BlockSpec((H_DIFF, t, DV_DIFF), lambda b, j: (0, b, 0))
    past = pl.BlockSpec((None, None, tk * H_DIFF, DV_DIFF),
                        lambda b, j: (layer, b, jnp.minimum(j, nk - 1), 0))
    kern = functools.partial(_attn_sample_kernel, t=t, tk=tk, past_len=past_len, lam_init=lam_init)
    per_head = lambda shape, dtype: pltpu.VMEM((H_DIFF,) + shape, dtype)
    return pl.pallas_call(
        kern,
        grid=(bsz, nk + 1),
        in_specs=[new(1), fresh, fresh, new(2), past, past,
                  vec(DK_DIFF), vec(DK_DIFF), vec(DK_DIFF), vec(DK_DIFF), vec(DV_DIFF)],
        out_specs=pl.BlockSpec((t, C_DIFF), lambda b, j: (b, 0)),
        out_shape=jax.ShapeDtypeStruct((bsz * t, C_DIFF), BF16),
        scratch_shapes=[
            per_head((2 * t, DV_DIFF), BF16),
            per_head((2 * t, tk), F32),
            per_head((2 * t, tk), BF16),
            per_head((2 * t, LANES), F32), per_head((2 * t, LANES), F32),
            per_head((2 * t, LANES), F32),
            per_head((2 * t, 2 * DV_DIFF), F32),
        ],
        compiler_params=_cparams(("parallel", "arbitrary")),
        name="attn_sample",
    )(gq, k16, v16, gq, cache_k, cache_v, lq1, lk1, lq2, lk2, subln_g)


def kernel(x_prompt, x_sample, cache_k, cache_v, state_wkv, state_shift, norm_g, w_in, shift_mu, w_decay0, w_decay2,
           a0, w_a2, k_k, k_a, r_k, lnx_w, lnx_b, lam_q1, lam_k1, lam_q2, lam_k2, subln_g, w_out, final_g):
    depth = w_in.shape[0]
    bp, tp, d = x_prompt.shape
    bs, ts, _ = x_sample.shape
    assert bp == 1, "the prompt attention kernel handles one sequence"

    xp = x_prompt.reshape(bp * tp, d)
    xs = x_sample.reshape(bs * ts, d)
    w_sh = w_in[:, :, :D_SHIFT].astype(BF16)
    w_rest = w_in[:, :, D_SHIFT:].astype(BF16)
    w_out_b = w_out.astype(BF16)
    zero_state = jnp.zeros((bp, H_RWKV, RWKV_HEAD, RWKV_HEAD), state_wkv.dtype)
    zero_shift = jnp.zeros((bp, 1, D_SHIFT), state_shift.dtype)
    row = lambda a: a.reshape(1, -1)

    outs = {name: [] for name in ("sp", "hp", "ss", "hs")}
    kv_p, kv_s = [None], [None]
    for l in range(depth):
        lam_init = _lambda_init(l)
        g = row(norm_g[l])
        rw = (row(shift_mu[l]), row(w_decay0[l]), w_decay2[l], row(a0[l]), w_a2[l], row(k_k[l]), row(k_a[l]),
              row(r_k[l]), row(lnx_w[l]), row(lnx_b[l]))
        lam = (row(lam_q1[l]), row(lam_k1[l]), row(lam_q2[l]), row(lam_k2[l]), row(subln_g[l]))

        def stream(x, bsz, kv, s0, prev0, attn):
            sh = _norm_matmul(x, g, w_sh, l, 1024, 640)
            gq, k_all, v_all, k16, v16 = _proj_rest(x, g, w_rest, kv[0], l, 512)
            kv[0] = (k_all, v_all)
            y_r, s_t = _rwkv_mix(sh, gq, s0, prev0, *rw)
            y_d = attn(gq, k16, v16)
            x = _out_matmul(x, y_r, y_d, w_out_b, l, 512, 1024)
            shift = sh.reshape(bsz, sh.shape[0] // bsz, D_SHIFT)[:, -1]
            return x, s_t, shift

        xp, sn, hn = stream(
            xp, bp, kv_p, zero_state, zero_shift,
            lambda gq, k16, v16: _attn_prompt(gq, k16, v16, *lam, lam_init, 512))
        outs["sp"].append(sn); outs["hp"].append(hn)
        xs, sn, hn = stream(
            xs, bs, kv_s, state_wkv[l], state_shift[l].reshape(bs, 1, D_SHIFT),
            lambda gq, k16, v16: _attn_sample(gq, k16, v16, cache_k, cache_v, l, *lam, lam_init, 1024))
        outs["ss"].append(sn); outs["hs"].append(hn)

    yp = _final_norm(xp, row(final_g), 512).reshape(bp, tp, d)
    ys = _final_norm(xs, row(final_g), 512).reshape(bs, ts, d)
    st = lambda name: jnp.stack(outs[name])
    heads = lambda a, bsz, t: a.reshape(depth, bsz, t, H_DIFF, DV_DIFF)
    (kp, vp), (ks, vs) = kv_p[0], kv_s[0]
    return (yp, ys, heads(kp, bp, tp), heads(vp, bp, tp), st("sp"), st("hp"),
            heads(ks, bs, ts), heads(vs, bs, ts), st("ss"), st("hs"))
```
